```python
import math
import jax, jax.numpy as jnp
from jax import lax
import numpy as np

D_MODEL = 1024
BATCH = 4
SEQ = 4096
DEPTH = 4

CHUNK = 64
N_META = 16
META_PAD = CHUNK - N_META

A_HEADS = 6
A_DK = 64
A_DV = 64
A_WIDTH = A_HEADS * A_DV

B_HEADS = 6
B_HEADDIM = 64
B_WIDTH = B_HEADS * B_HEADDIM
B_GROUPS = 2
B_DSTATE = 128
B_CONV = 4
B_CONV_DIM = B_WIDTH + 2 * B_GROUPS * B_DSTATE

C_GROUPS = 16
C_GROUP_CH = 16
C_WIDTH = C_GROUPS * C_GROUP_CH
C_STATE = 64

D_MIX = A_WIDTH + B_WIDTH + C_WIDTH
D_FF = 4 * D_MODEL
SPLITS = (A_WIDTH, A_WIDTH, A_WIDTH, A_WIDTH, B_WIDTH, B_CONV_DIM, B_HEADS, C_WIDTH)
D_IN = sum(SPLITS)
ALPHA = (2 * DEPTH) ** 0.25
BETA = (8 * DEPTH) ** -0.25
LN_EPS = 1e-5
RMS_EPS = 1e-6
S5_MAX_RE = -1e-4

kernel_name = 'hybrid_hgrn2_ssd_s5_deepnorm'


def layer_norm(x, g, b):
    xf = x.astype(jnp.float32)
    mu = jnp.mean(xf, axis=-1, keepdims=True)
    var = jnp.mean(jnp.square(xf - mu), axis=-1, keepdims=True)
    y = (xf - mu) * lax.rsqrt(var + LN_EPS) * g.astype(jnp.float32) + b.astype(jnp.float32)
    return y.astype(x.dtype)


def rms_norm(x):
    xf = x.astype(jnp.float32)
    return xf * lax.rsqrt(jnp.mean(jnp.square(xf), axis=-1, keepdims=True) + RMS_EPS)


def pad_front(t):
    pad = [(0, 0)] * t.ndim
    pad[1] = (META_PAD, 0)
    return jnp.pad(t, pad)


def hgrn2_mixer(q_raw, f_raw, i_raw, g_raw, lb, norm_w):
    f32 = jnp.float32
    bsz, seq_len, _ = q_raw.shape
    zf = f_raw.astype(f32)
    q = jax.nn.silu(q_raw.astype(f32))
    log_f = jnp.logaddexp(jax.nn.log_sigmoid(zf), jnp.log(lb) + jax.nn.log_sigmoid(-zf))
    k = (1.0 - lb) * jax.nn.sigmoid(-zf)
    v = i_raw.astype(f32)

    def to_chunks(t, d):
        t = pad_front(t).reshape(bsz, -1, CHUNK, A_HEADS, d)
        return t.transpose(1, 0, 3, 2, 4)

    causal = jnp.tril(jnp.ones((CHUNK, CHUNK), dtype=bool))

    def chunk_step(state, inp):
        qc, kc, vc, gc = inp
        G = jnp.cumsum(gc, axis=2)
        o_inter = jnp.einsum('bhtk,bhkv->bhtv', qc * jnp.exp(G), state)
        diff = G[:, :, :, None, :] - G[:, :, None, :, :]
        decay = jnp.exp(jnp.where(causal[:, :, None], diff, -jnp.inf))
        scores = jnp.einsum('bhtsk,bhsk->bhts', qc[:, :, :, None, :] * decay, kc)
        o_intra = jnp.einsum('bhts,bhsv->bhtv', scores, vc)
        G_last = G[:, :, -1:, :]
        new_state = (jnp.exp(G_last[:, :, 0, :])[..., None] * state
                     + jnp.einsum('bhsk,bhsv->bhkv', kc * jnp.exp(G_last - G), vc))
        return new_state, o_inter + o_intra

    state0 = jnp.zeros((bsz, A_HEADS, A_DK, A_DV), f32)
    _, o = lax.scan(chunk_step, state0,
                    (to_chunks(q, A_DK), to_chunks(k, A_DK), to_chunks(v, A_DV), to_chunks(log_f, A_DK)))
    o = o.transpose(1, 0, 3, 2, 4).reshape(bsz, -1, A_HEADS, A_DV)[:, META_PAD:]
    gate = jax.nn.silu(g_raw.astype(f32)).reshape(bsz, seq_len, A_HEADS, A_DV)
    o = rms_norm(o) * norm_w.astype(f32) * gate
    return o.reshape(bsz, seq_len, A_WIDTH)


def causal_depthwise_conv(x, w, b):
    out = lax.conv_general_dilated(x, w[:, None, :], window_strides=(1,),
                                   padding=[(B_CONV - 1, 0)],
                                   dimension_numbers=('NWC', 'WIO', 'NWC'),
                                   feature_group_count=x.shape[-1])
    return out + b


def segsum(a):
    T = a.shape[-1]
    cs = jnp.cumsum(a, axis=-1)
    diff = cs[..., :, None] - cs[..., None, :]
    mask = jnp.tril(jnp.ones((T, T), dtype=bool))
    return jnp.where(mask, diff, -jnp.inf)


def ssd_chunked(xdt, dA, b_h, c_h):
    a_cum = jnp.cumsum(dA, axis=-1)
    l_mat = jnp.exp(segsum(dA))
    scores = jnp.einsum('bclhn,bcshn->bhcls', c_h, b_h) * l_mat
    y_diag = jnp.einsum('bhcls,bcshp->bclhp', scores, xdt)
    decay_states = jnp.exp(a_cum[..., -1:] - a_cum)
    states = jnp.einsum('bclhn,bhcl,bclhp->bchpn', b_h, decay_states, xdt)
    states = jnp.concatenate([jnp.zeros_like(states[:, :1]), states], axis=1)
    chunk_decay = jnp.exp(segsum(jnp.pad(a_cum[..., -1], ((0, 0), (0, 0), (1, 0)))))
    new_states = jnp.einsum('bhzc,bchpn->bzhpn', chunk_decay, states)
    prev_states = new_states[:, :-1]
    y_off = jnp.einsum('bclhn,bchpn,bhcl->bclhp', c_h, prev_states, jnp.exp(a_cum))
    return y_diag + y_off


def mamba2_mixer(z, xbc, dt_raw, conv_w, conv_b, dt_bias, a_log, d_skip, norm_w):
    f32 = jnp.float32
    bsz, seq_len, _ = z.shape
    xbc = jax.nn.silu(causal_depthwise_conv(xbc.astype(f32), conv_w.astype(f32), conv_b.astype(f32)))
    xs, b_in, c_in = jnp.split(xbc, [B_WIDTH, B_WIDTH + B_GROUPS * B_DSTATE], axis=-1)
    dt = jax.nn.softplus(dt_raw.astype(f32) + dt_bias.astype(f32))
    a = -jnp.exp(a_log.astype(f32))
    nc = (seq_len + META_PAD) // CHUNK
    rep = B_HEADS // B_GROUPS
    xs = pad_front(xs).reshape(bsz, nc, CHUNK, B_HEADS, B_HEADDIM)
    b_h = jnp.repeat(pad_front(b_in).reshape(bsz, nc, CHUNK, B_GROUPS, B_DSTATE), rep, axis=3)
    c_h = jnp.repeat(pad_front(c_in).reshape(bsz, nc, CHUNK, B_GROUPS, B_DSTATE), rep, axis=3)
    dt = pad_front(dt).reshape(bsz, nc, CHUNK, B_HEADS)
    dA = (dt * a).transpose(0, 3, 1, 2)
    y = ssd_chunked(xs * dt[..., None], dA, b_h, c_h) + d_skip.astype(f32)[:, None] * xs
    y = y.reshape(bsz, -1, B_WIDTH)[:, META_PAD:]
    y = y * jax.nn.silu(z.astype(f32))
    y = rms_norm(y.reshape(bsz, seq_len, B_GROUPS, B_WIDTH // B_GROUPS)).reshape(bsz, seq_len, B_WIDTH)
    return y * norm_w.astype(f32)


def s5_mixer(u, a_re, a_im, log_dt, b_re, b_im, c_re, c_im, d_skip, glu_w, glu_b):
    f32 = jnp.float32
    bsz, seq_len, _ = u.shape
    uf = u.astype(f32)
    lam_re = jnp.minimum(a_re.astype(f32), S5_MAX_RE)
    lam_im = a_im.astype(f32)
    dt = jnp.exp(log_dt.astype(f32))[:, None]
    mag = jnp.exp(lam_re * dt)
    lb_re = mag * jnp.cos(lam_im * dt)
    lb_im = mag * jnp.sin(lam_im * dt)
    den = jnp.square(lam_re) + jnp.square(lam_im)
    nr = lb_re - 1.0
    s_re = (nr * lam_re + lb_im * lam_im) / den
    s_im = (lb_im * lam_re - nr * lam_im) / den
    br = b_re.astype(f32)
    bi = b_im.astype(f32)
    bb_re = s_re[..., None] * br - s_im[..., None] * bi
    bb_im = s_re[..., None] * bi + s_im[..., None] * br
    ug = uf.reshape(bsz, seq_len, C_GROUPS, C_GROUP_CH)
    bu_re = jnp.einsum('blgc,gnc->blgn', ug, bb_re)
    bu_im = jnp.einsum('blgc,gnc->blgn', ug, bb_im)
    a_el_re = jnp.broadcast_to(lb_re, bu_re.shape)
    a_el_im = jnp.broadcast_to(lb_im, bu_im.shape)

    def combine(e1, e2):
        a1r, a1i, b1r, b1i = e1
        a2r, a2i, b2r, b2i = e2
        return (a2r * a1r - a2i * a1i, a2r * a1i + a2i * a1r,
                a2r * b1r - a2i * b1i + b2r, a2r * b1i + a2i * b1r + b2i)

    _, _, x_re, x_im = lax.associative_scan(combine, (a_el_re, a_el_im, bu_re, bu_im), axis=1)
    y = (jnp.einsum('blgn,gcn->blgc', x_re, c_re.astype(f32))
         - jnp.einsum('blgn,gcn->blgc', x_im, c_im.astype(f32)))
    y = y.reshape(bsz, seq_len, C_WIDTH) + d_skip.astype(f32) * uf
    y = jax.nn.gelu(y)
    return y * jax.nn.sigmoid(y @ glu_w.astype(f32) + glu_b.astype(f32))


def setup_inputs(seed: int = 0) -> dict:
    key = jax.random.key(seed)
    ks = jax.random.split(key, 32)
    f32 = jnp.float32

    def nrm(k, shape, scale):
        return scale * jax.random.normal(k, shape, f32)

    dt0 = jnp.exp(jax.random.uniform(ks[6], (DEPTH, B_HEADS), f32, math.log(1e-3), math.log(1e-1)))
    s5_a_im = (math.pi * jnp.arange(C_STATE, dtype=f32))[None, None, :] + nrm(ks[11], (DEPTH, C_GROUPS, C_STATE), 0.01)
    return {
        'x': nrm(ks[0], (BATCH, SEQ, D_MODEL), 1.0),
        'meta_tokens': nrm(ks[1], (N_META, D_MODEL), 1.0),
        'w_in': nrm(ks[2], (DEPTH, D_MODEL, D_IN), D_MODEL ** -0.5),
        'hgrn_lb_logits': nrm(ks[3], (DEPTH, A_WIDTH), 0.1),
        'hgrn_norm_w': 1.0 + nrm(ks[4], (DEPTH, A_DV), 0.02),
        'm2_conv_w': nrm(ks[5], (DEPTH, B_CONV, B_CONV_DIM), B_CONV ** -0.5),
        'm2_conv_b': nrm(ks[7], (DEPTH, B_CONV_DIM), 0.02),
        'm2_dt_bias': dt0 + jnp.log(-jnp.expm1(-dt0)),
        'm2_a_log': jnp.log(jax.random.uniform(ks[8], (DEPTH, B_HEADS), f32, 1.0, 16.0)),
        'm2_d': 1.0 + nrm(ks[9], (DEPTH, B_HEADS), 0.1),
        'm2_norm_w': 1.0 + nrm(ks[10], (DEPTH, B_WIDTH), 0.02),
        's5_a_re': -0.5 + nrm(ks[12], (DEPTH, C_GROUPS, C_STATE), 0.01),
        's5_a_im': s5_a_im,
        's5_log_dt': jax.random.uniform(ks[13], (DEPTH, C_GROUPS), f32, math.log(1e-3), math.log(1e-1)),
        's5_b_re': nrm(ks[14], (DEPTH, C_GROUPS, C_STATE, C_GROUP_CH), (2 * C_GROUP_CH) ** -0.5),
        's5_b_im': nrm(ks[15], (DEPTH, C_GROUPS, C_STATE, C_GROUP_CH), (2 * C_GROUP_CH) ** -0.5),
        's5_c_re': nrm(ks[16], (DEPTH, C_GROUPS, C_GROUP_CH, C_STATE), C_STATE ** -0.5),
        's5_c_im': nrm(ks[17], (DEPTH, C_GROUPS, C_GROUP_CH, C_STATE), C_STATE ** -0.5),
        's5_d': nrm(ks[18], (DEPTH, C_WIDTH), 1.0),
        's5_glu_w': nrm(ks[19], (DEPTH, C_WIDTH, C_WIDTH), C_WIDTH ** -0.5),
        's5_glu_b': nrm(ks[20], (DEPTH, C_WIDTH), 0.02),
        'w_out': nrm(ks[21], (DEPTH, D_MIX, D_MODEL), BETA * D_MIX ** -0.5),
        'ln1_g': 1.0 + nrm(ks[22], (DEPTH, D_MODEL), 0.02),
        'ln1_b': nrm(ks[23], (DEPTH, D_MODEL), 0.02),
        'w_mlp_in': nrm(ks[24], (DEPTH, D_MODEL, D_FF), D_MODEL ** -0.5),
        'w_mlp_out': nrm(ks[25], (DEPTH, D_FF, D_MODEL), BETA * D_FF ** -0.5),
        'ln2_g': 1.0 + nrm(ks[26], (DEPTH, D_MODEL), 0.02),
        'ln2_b': nrm(ks[27], (DEPTH, D_MODEL), 0.02),
    }


def reference(x, meta_tokens, w_in, hgrn_lb_logits, hgrn_norm_w, m2_conv_w, m2_conv_b,
              m2_dt_bias, m2_a_log, m2_d, m2_norm_w, s5_a_re, s5_a_im, s5_log_dt,
              s5_b_re, s5_b_im, s5_c_re, s5_c_im, s5_d, s5_glu_w, s5_glu_b, w_out,
              ln1_g, ln1_b, w_mlp_in, w_mlp_out, ln2_g, ln2_b):
    bsz = x.shape[0]
    meta = jnp.broadcast_to(meta_tokens.astype(x.dtype)[None], (bsz, N_META, D_MODEL))
    h = jnp.concatenate([meta, x], axis=1)
    lb_cum = jnp.cumsum(jax.nn.softmax(hgrn_lb_logits.astype(jnp.float32), axis=0), axis=0)
    lower_bounds = lb_cum - lb_cum[0]
    split_idx = np.cumsum(SPLITS)[:-1].tolist()
    for l in range(DEPTH):
        proj = h @ w_in[l]
        q_a, f_a, i_a, g_a, z_b, xbc_b, dt_b, u_c = jnp.split(proj, split_idx, axis=-1)
        y_a = hgrn2_mixer(q_a, f_a, i_a, g_a, lower_bounds[l], hgrn_norm_w[l])
        y_b = mamba2_mixer(z_b, xbc_b, dt_b, m2_conv_w[l], m2_conv_b[l], m2_dt_bias[l],
                           m2_a_log[l], m2_d[l], m2_norm_w[l])
        y_c = s5_mixer(u_c, s5_a_re[l], s5_a_im[l], s5_log_dt[l], s5_b_re[l], s5_b_im[l],
                       s5_c_re[l], s5_c_im[l], s5_d[l], s5_glu_w[l], s5_glu_b[l])
        mixed = jnp.concatenate([y_a, y_b, y_c], axis=-1).astype(h.dtype) @ w_out[l]
        h = layer_norm(ALPHA * h + mixed, ln1_g[l], ln1_b[l])
        ff = jnp.square(jax.nn.relu(h @ w_mlp_in[l])) @ w_mlp_out[l]
        h = layer_norm(ALPHA * h + ff, ln2_g[l], ln2_b[l])
    return h[:, N_META:]
```

```python
import functools
import math

import jax
import jax.numpy as jnp
from jax import lax
from jax.experimental import pallas as pl
from jax.experimental.pallas import tpu as pltpu

F32 = jnp.float32
BF16 = jnp.bfloat16

D_MODEL = 1024
N_META = 16
META_PAD = 48
A_HEADS, A_DK, A_W = 6, 64, 384
B_HEADS, B_P, B_W, B_G, B_N = 6, 64, 384, 2, 128
B_CONV = 4
C_G, C_CH, C_W, C_N = 16, 16, 256, 64
C_S = C_G * C_N
D_FF = 4 * D_MODEL
LN_EPS = 1e-5
RMS_EPS = 1e-6
S5_MAX_RE = -1e-4

O_Q, O_F, O_I, O_G, O_Z, O_XBC, O_U, O_DT = 0, 384, 768, 1152, 1536, 1920, 2816, 3072
D_IN_PACKED = 3200
XBC_W = 896

ROW_BLOCK = 320
MLP_ROW_BLOCK = 640
LANE = 128
SUBLANE = 8
VMEM_LIMIT = 56 * 1024 * 1024


def _sigmoid(x):
    return 1.0 / (1.0 + jnp.exp(-x))


def _silu(x):
    return x * _sigmoid(x)


def _softplus(x):
    return jnp.maximum(x, 0.0) + jnp.log1p(jnp.exp(-jnp.abs(x)))


def _split_bf16(x, parts):
    out = []
    r = x
    for i in range(parts):
        p = r.astype(BF16)
        out.append(p)
        if i + 1 < parts:
            r = r - p.astype(F32)
    return out


def _dot(a, b):
    return jnp.dot(a, b, preferred_element_type=F32)


def _dot_nt(a, b):
    return lax.dot_general(a, b, (((1,), (1,)), ((), ())), preferred_element_type=F32)


def _dot_tn(a, b):
    return lax.dot_general(a, b, (((0,), (0,)), ((), ())), preferred_element_type=F32)


def _shift_rows(x, n):
    rows = x.shape[0]
    if n == 0:
        return x
    if n % SUBLANE == 0:
        z = jnp.zeros((abs(n), x.shape[1]), x.dtype)
        if n > 0:
            return jnp.concatenate([z, x[: rows - n]], axis=0)
        return jnp.concatenate([x[-n:], z], axis=0)
    return pltpu.roll(x, n % rows, axis=0)


def _layer_norm(x, g, b):
    mu = jnp.mean(x, axis=-1, keepdims=True)
    xc = x - mu
    var = jnp.mean(xc * xc, axis=-1, keepdims=True)
    return xc * lax.rsqrt(var + LN_EPS) * g + b


def _hgrn2(q_raw, f_raw, v, g_raw, log_lb, one_m_lb, norm_w, pad_row, st_ref):
    rows = q_raw.shape[0]
    n_top = rows // 64
    q = _silu(q_raw)
    lse = jnp.log1p(jnp.exp(-jnp.abs(f_raw)))
    ls_pos = -(jnp.maximum(-f_raw, 0.0) + lse)
    ls_neg = -(jnp.maximum(f_raw, 0.0) + lse)
    b_term = log_lb + ls_neg
    lf = jnp.maximum(ls_pos, b_term) + jnp.log1p(jnp.exp(-jnp.abs(ls_pos - b_term)))
    lf = jnp.where(pad_row, 0.0, lf)
    k = jnp.where(pad_row, 0.0, one_m_lb * jnp.exp(ls_neg))

    row = lax.broadcasted_iota(jnp.int32, (rows, 1), 0)

    cs, sf, tot = {1: lf}, {1: jnp.zeros_like(lf)}, {1: lf}
    for c in (1, 4, 16):
        j = (row // c) % 4
        acc_cs, acc_sf = cs[c], sf[c]
        for m in (1, 2, 3):
            acc_cs = acc_cs + jnp.where(j >= m, _shift_rows(tot[c], m * c), 0.0)
            acc_sf = acc_sf + jnp.where(j <= 3 - m, _shift_rows(tot[c], -m * c), 0.0)
        cs[4 * c], sf[4 * c] = acc_cs, acc_sf
        tot[4 * c] = acc_cs + acc_sf
    j_top = row // 64
    g_in, r_out = cs[64], sf[64]
    for m in range(1, n_top):
        g_in = g_in + jnp.where(j_top >= m, _shift_rows(tot[64], 64 * m), 0.0)
        r_out = r_out + jnp.where(j_top <= n_top - 1 - m, _shift_rows(tot[64], -64 * m), 0.0)

    lane = lax.broadcasted_iota(jnp.int32, (1, LANE), 1)
    lo = lane < 64

    def level_operands(c, n_groups):
        j = (row // c) % 4 if c < 64 else j_top
        n_slots = n_groups - 1
        qt = q * jnp.exp(cs[c])
        e = sf[c]
        ke = []
        for d in range(1, n_slots + 1):
            if d > 1:
                e = e + _shift_rows(tot[c], -(d - 1) * c)
            ke.append(k * jnp.exp(e))
        q_slots, k_slots = [], []
        for i in range(1, n_slots + 1):
            q_slots.append(jnp.where(j == i, qt, 0.0).astype(BF16))
            ks = jnp.zeros_like(k)
            for d in range(1, i + 1):
                ks = jnp.where(j == i - d, ke[d - 1], ks)
            k_slots.append(ks.astype(BF16))
        return q_slots, k_slots

    levels = [level_operands(64, n_top), level_operands(16, 4), level_operands(4, 4), level_operands(1, 4)]

    tt = lax.broadcasted_iota(jnp.int32, (rows, 2 * rows), 0)
    ss = lax.broadcasted_iota(jnp.int32, (rows, 2 * rows), 1)
    ss = jnp.where(ss >= rows, ss - rows, ss)
    same64 = (tt // 64) == (ss // 64)
    same16 = (tt // 16) == (ss // 16)
    same4 = (tt // 4) == (ss // 4)
    diag = tt == ss

    q_in = (q * jnp.exp(g_in)).astype(BF16)
    k_out = (k * jnp.exp(r_out)).astype(BF16)
    q_b = q.astype(BF16)
    k_b = k.astype(BF16)
    v_b = v.astype(BF16)
    decay_blk = jnp.exp(g_in[rows - 1:rows, :])

    r2 = lax.broadcasted_iota(jnp.int32, (LANE, LANE), 0)
    c2 = lax.broadcasted_iota(jnp.int32, (LANE, LANE), 1)
    head_diag = (r2 // 64) == (c2 // 64)

    def two_heads(x):
        lo_t = jnp.concatenate([lo] * (x.shape[1] // LANE), axis=1)
        zero = jnp.zeros_like(x)
        return jnp.concatenate([jnp.where(lo_t, x, zero), jnp.where(lo_t, zero, x)], axis=0)

    outs = []
    for p in range(A_HEADS // 2):
        sl = slice(LANE * p, LANE * (p + 1))
        s_lvl = []
        for q_slots, k_slots in levels:
            qc = jnp.concatenate([x[:, sl] for x in q_slots], axis=1)
            kc = jnp.concatenate([x[:, sl] for x in k_slots], axis=1)
            s_lvl.append(_dot_nt(qc, two_heads(kc)))
        s_d = _dot_nt(q_b[:, sl], two_heads(k_b[:, sl]))
        s64, s16, s4, s1 = s_lvl
        scores = jnp.where(same4, jnp.where(diag, s_d, s1),
                           jnp.where(same16, s4, jnp.where(same64, s16, s64)))
        st = st_ref[p]
        o = _dot(scores.astype(BF16), two_heads(v_b[:, sl])) + _dot_nt(q_in[:, sl], st.astype(BF16))
        upd = _dot_tn(v_b[:, sl], k_out[:, sl])
        st_ref[p] = st * decay_blk[:, sl] + jnp.where(head_diag, upd, 0.0)
        outs.append(o)
    o = jnp.concatenate(outs, axis=1)

    r3 = lax.broadcasted_iota(jnp.int32, (A_W, A_W), 0)
    c3 = lax.broadcasted_iota(jnp.int32, (A_W, A_W), 1)
    head_ones = jnp.where((r3 // 64) == (c3 // 64), 1.0, 0.0).astype(BF16)
    ms = _dot((o * o).astype(BF16), head_ones) * (1.0 / A_DK)
    return o * lax.rsqrt(ms + RMS_EPS) * norm_w * _silu(g_raw)


def _ssd(z_raw, xbc_raw, dt_raw, conv_w, conv_b, dt_bias, a_log, d_full, norm_w, pad_row,
         hist_ref, st_ref):
    rows = z_raw.shape[0]
    cat = jnp.concatenate([hist_ref[...], xbc_raw], axis=0)
    hist_ref[...] = xbc_raw[rows - SUBLANE:, :]
    acc = conv_b + conv_w[B_CONV - 1:B_CONV, :] * xbc_raw
    for n in (1, 2, 3):
        shifted = pltpu.roll(cat, n, axis=0)[SUBLANE:, :]
        acc = acc + conv_w[B_CONV - 1 - n:B_CONV - n, :] * shifted
    xbc = _silu(acc)
    xs = xbc[:, :B_W]
    bm = xbc[:, B_W:B_W + B_G * B_N].astype(BF16)
    cm = xbc[:, B_W + B_G * B_N:].astype(BF16)

    lane = lax.broadcasted_iota(jnp.int32, (1, LANE), 1)
    dt = jnp.where(pad_row, 0.0, _softplus(dt_raw + dt_bias))
    a_neg = jnp.where(lane < B_HEADS, -jnp.exp(a_log), 0.0)
    d_a = dt * a_neg

    rr = lax.broadcasted_iota(jnp.int32, (rows, rows), 0)
    cc = lax.broadcasted_iota(jnp.int32, (rows, rows), 1)
    causal = cc <= rr
    tril = jnp.where(causal, 1.0, 0.0).astype(BF16)
    d_a_parts = _split_bf16(d_a, 3)
    cum = sum(_dot(tril, part) for part in d_a_parts)
    cum_t = sum(_dot_tn(part, jnp.where(rr <= cc, 1.0, 0.0).astype(BF16)) for part in d_a_parts)

    er = lax.broadcasted_iota(jnp.int32, (LANE, B_W), 0)
    ec = lax.broadcasted_iota(jnp.int32, (LANE, B_W), 1)
    expand = jnp.where(er == ec // B_P, 1.0, 0.0).astype(BF16)
    dt_full = sum(_dot(part, expand) for part in _split_bf16(dt, 3))
    cum_full = sum(_dot(part, expand) for part in _split_bf16(cum, 3))

    xdt = xs * dt_full
    lo = lane < 64
    scores = []
    cb = [_dot_nt(cm[:, B_N * g:B_N * (g + 1)], bm[:, B_N * g:B_N * (g + 1)]) for g in range(B_G)]
    for h in range(B_HEADS):
        diff = cum[:, h:h + 1] - cum_t[h:h + 1, :]
        decay = jnp.exp(jnp.where(causal, diff, -jnp.inf))
        scores.append((cb[h // (B_HEADS // B_G)] * decay).astype(BF16))
    y_parts = []
    for p in range(B_HEADS // 2):
        xp = xdt[:, LANE * p:LANE * (p + 1)]
        zero = jnp.zeros_like(xp)
        y_parts.append(_dot(scores[2 * p], jnp.where(lo, xp, zero).astype(BF16))
                       + _dot(scores[2 * p + 1], jnp.where(lo, zero, xp).astype(BF16)))
    y = jnp.concatenate(y_parts, axis=1)

    st = st_ref[...]
    y = y + _dot(cm, st.astype(BF16)) * jnp.exp(cum_full) + d_full * xs
    cum_last = cum_full[rows - 1:rows, :]
    upd = _dot_tn(bm, (xdt * jnp.exp(cum_last - cum_full)).astype(BF16))
    sr = lax.broadcasted_iota(jnp.int32, (B_G * B_N, B_W), 0)
    sc = lax.broadcasted_iota(jnp.int32, (B_G * B_N, B_W), 1)
    st_ref[...] = st * jnp.exp(cum_last) + jnp.where((sr // B_N) == (sc // (B_W // B_G)), upd, 0.0)

    y = y * _silu(z_raw)
    gr = lax.broadcasted_iota(jnp.int32, (B_W, B_W), 0)
    gc = lax.broadcasted_iota(jnp.int32, (B_W, B_W), 1)
    group_ones = jnp.where((gr // (B_W // B_G)) == (gc // (B_W // B_G)), 1.0, 0.0).astype(BF16)
    ms = _dot((y * y).astype(BF16), group_ones) * (1.0 / (B_W // B_G))
    return y * lax.rsqrt(ms + RMS_EPS) * norm_w


def _s5(u, b_bd, c_bd, pow_re, pow_im, d_skip, glu_w, glu_b, x_re_ref, x_im_ref, carry_ref):
    rows = u.shape[0]
    bu = _dot(u.astype(BF16), b_bd)
    x_re, x_im = bu[:, :C_S], bu[:, C_S:]
    sub = lax.broadcasted_iota(jnp.int32, (rows, 1), 0) % SUBLANE
    for d in (1, 2, 4):
        p_re, p_im = pow_re[d - 1:d, :], pow_im[d - 1:d, :]
        keep = sub >= d
        s_re = jnp.where(keep, pltpu.roll(x_re, d, axis=0), 0.0)
        s_im = jnp.where(keep, pltpu.roll(x_im, d, axis=0), 0.0)
        x_re, x_im = (x_re + (p_re * s_re - p_im * s_im), x_im + (p_re * s_im + p_im * s_re))
    x_re_ref[...] = x_re
    x_im_ref[...] = x_im

    def carry_step(i, carry):
        c_re, c_im = carry
        r0 = pl.multiple_of(i * SUBLANE, SUBLANE)
        g_re = x_re_ref[pl.ds(r0, SUBLANE), :] + (pow_re * c_re - pow_im * c_im)
        g_im = x_im_ref[pl.ds(r0, SUBLANE), :] + (pow_re * c_im + pow_im * c_re)
        x_re_ref[pl.ds(r0, SUBLANE), :] = g_re
        x_im_ref[pl.ds(r0, SUBLANE), :] = g_im
        return g_re[SUBLANE - 1:, :], g_im[SUBLANE - 1:, :]

    c_re, c_im = lax.fori_loop(0, rows // SUBLANE, carry_step, (carry_ref[0], carry_ref[1]))
    carry_ref[0] = c_re
    carry_ref[1] = c_im

    y = (_dot(x_re_ref[...].astype(BF16), c_bd[:C_S, :]) + _dot(x_im_ref[...].astype(BF16), c_bd[C_S:, :])
         + d_skip * u)
    y = jax.nn.gelu(y, approximate=True)
    return y * _sigmoid(_dot(y.astype(BF16), glu_w) + glu_b)


def _mixer_kernel(alpha, h_ref, w_in_ref, log_lb_ref, one_m_lb_ref, a_norm_ref,
                  conv_w_ref, conv_b_ref, dt_bias_ref, a_log_ref, d_full_ref, b_norm_ref,
                  s5_b_ref, s5_c_ref, s5_pre_ref, s5_pim_ref, s5_d_ref, glu_w_ref, glu_b_ref,
                  w_out_ref, ln_g_ref, ln_b_ref, o_ref,
                  a_state, b_hist, b_state, c_re, c_im, c_carry):
    blk = pl.program_id(1)
    rows = h_ref.shape[1]

    @pl.when(blk == 0)
    def _():
        a_state[...] = jnp.zeros_like(a_state)
        b_hist[...] = jnp.zeros_like(b_hist)
        b_state[...] = jnp.zeros_like(b_state)
        c_carry[...] = jnp.zeros_like(c_carry)

    row = lax.broadcasted_iota(jnp.int32, (rows, 1), 0) + blk * rows
    pad_row = row < META_PAD
    h = h_ref[0]
    proj = _dot(jnp.where(pad_row, 0.0, h).astype(BF16), w_in_ref[...])

    y_a = _hgrn2(proj[:, O_Q:O_F], proj[:, O_F:O_I], proj[:, O_I:O_G], proj[:, O_G:O_Z],
                 log_lb_ref[...], one_m_lb_ref[...], a_norm_ref[...], pad_row, a_state)
    y_b = _ssd(proj[:, O_Z:O_XBC], proj[:, O_XBC:O_U], proj[:, O_DT:], conv_w_ref[...], conv_b_ref[...],
               dt_bias_ref[...], a_log_ref[...], d_full_ref[...], b_norm_ref[...], pad_row, b_hist, b_state)
    y_c = _s5(proj[:, O_U:O_DT], s5_b_ref[...], s5_c_ref[...], s5_pre_ref[...], s5_pim_ref[...],
              s5_d_ref[...], glu_w_ref[...], glu_b_ref[...], c_re, c_im, c_carry)

    mixed = (_dot(y_a.astype(BF16), w_out_ref[:A_W, :]) + _dot(y_b.astype(BF16), w_out_ref[A_W:A_W + B_W, :])
             + _dot(y_c.astype(BF16), w_out_ref[A_W + B_W:, :]))
    o_ref[0] = _layer_norm(alpha * h + mixed, ln_g_ref[...], ln_b_ref[...])


def _const_spec(arr):
    nd = arr.ndim
    return pl.BlockSpec(arr.shape, lambda b, j, _nd=nd: (0,) * _nd)


def _mixer_call(hp, params, alpha):
    bsz, lp, d = hp.shape
    rows = ROW_BLOCK
    assert lp % rows == 0 and rows % 64 == 0
    h_spec = pl.BlockSpec((1, rows, d), lambda b, j: (b, j, 0))
    return pl.pallas_call(
        functools.partial(_mixer_kernel, alpha),
        grid=(bsz, lp // rows),
        in_specs=[h_spec] + [_const_spec(p) for p in params],
        out_specs=h_spec,
        out_shape=jax.ShapeDtypeStruct(hp.shape, F32),
        scratch_shapes=[
            pltpu.VMEM((A_HEADS // 2, LANE, LANE), F32),
            pltpu.VMEM((SUBLANE, XBC_W), F32),
            pltpu.VMEM((B_G * B_N, B_W), F32),
            pltpu.VMEM((rows, C_S), F32),
            pltpu.VMEM((rows, C_S), F32),
            pltpu.VMEM((2, 1, C_S), F32),
        ],
        compiler_params=pltpu.CompilerParams(
            dimension_semantics=("arbitrary", "arbitrary"), vmem_limit_bytes=VMEM_LIMIT),
        name="mixer",
    )(hp, *params)


def _mlp_kernel(alpha, h_ref, w1_ref, w2_ref, ln_g_ref, ln_b_ref, o_ref):
    h = h_ref[...]
    hb = h.astype(BF16)
    ff = jnp.zeros_like(h)
    step = D_MODEL
    for j in range(D_FF // step):
        hid = jnp.maximum(_dot(hb, w1_ref[:, j * step:(j + 1) * step]), 0.0)
        ff = ff + _dot((hid * hid).astype(BF16), w2_ref[j * step:(j + 1) * step, :])
    o_ref[...] = _layer_norm(alpha * h + ff, ln_g_ref[...], ln_b_ref[...])


def _mlp_call(h2, w1, w2, ln_g, ln_b, alpha):
    n, d = h2.shape
    rows = MLP_ROW_BLOCK
    assert n % rows == 0
    h_spec = pl.BlockSpec((rows, d), lambda i: (i, 0))

    def const(arr):
        return pl.BlockSpec(arr.shape, lambda i: (0, 0))

    return pl.pallas_call(
        functools.partial(_mlp_kernel, alpha),
        grid=(n // rows,),
        in_specs=[h_spec, const(w1), const(w2), const(ln_g), const(ln_b)],
        out_specs=h_spec,
        out_shape=jax.ShapeDtypeStruct(h2.shape, F32),
        compiler_params=pltpu.CompilerParams(
            dimension_semantics=("arbitrary",), vmem_limit_bytes=VMEM_LIMIT),
        name="mlp",
    )(h2, w1, w2, ln_g, ln_b)


def _block_diag(blocks):
    g, r, c = blocks.shape
    eye = jnp.eye(g, dtype=blocks.dtype)
    return (eye[:, None, :, None] * blocks[:, :, None, :]).reshape(g * r, g * c)


def _s5_params(a_re, a_im, log_dt, b_re, b_im, c_re, c_im):
    lam_re = jnp.minimum(a_re, S5_MAX_RE)
    lam_im = a_im
    dt = jnp.exp(log_dt)[:, None]
    mag = jnp.exp(lam_re * dt)
    lb_re = mag * jnp.cos(lam_im * dt)
    lb_im = mag * jnp.sin(lam_im * dt)
    den = jnp.square(lam_re) + jnp.square(lam_im)
    nr = lb_re - 1.0
    s_re = (nr * lam_re + lb_im * lam_im) / den
    s_im = (lb_im * lam_re - nr * lam_im) / den
    bb_re = s_re[..., None] * b_re - s_im[..., None] * b_im
    bb_im = s_re[..., None] * b_im + s_im[..., None] * b_re
    b_bd = jnp.concatenate([_block_diag(bb_re.transpose(0, 2, 1)), _block_diag(bb_im.transpose(0, 2, 1))], axis=1)
    c_bd = jnp.concatenate([_block_diag(c_re.transpose(0, 2, 1)), -_block_diag(c_im.transpose(0, 2, 1))], axis=0)
    steps = jnp.arange(1, SUBLANE + 1, dtype=F32)[:, None, None]
    pmag = jnp.exp(lam_re * dt * steps)
    pow_re = (pmag * jnp.cos(lam_im * dt * steps)).reshape(SUBLANE, C_S)
    pow_im = (pmag * jnp.sin(lam_im * dt * steps)).reshape(SUBLANE, C_S)
    return b_bd.astype(BF16), c_bd.astype(BF16), pow_re, pow_im


def kernel(x, meta_tokens, w_in, hgrn_lb_logits, hgrn_norm_w, m2_conv_w, m2_conv_b, m2_dt_bias, m2_a_log, m2_d, m2_norm_w, s5_a_re, s5_a_im, s5_log_dt, s5_b_re, s5_b_im, s5_c_re, s5_c_im, s5_d, s5_glu_w, s5_glu_b, w_out, ln1_g, ln1_b, w_mlp_in, w_mlp_out, ln2_g, ln2_b):
    bsz, seq, d = x.shape
    depth = w_in.shape[0]
    alpha = (2 * depth) ** 0.25
    lp = META_PAD + N_META + seq

    meta = jnp.broadcast_to(meta_tokens.astype(x.dtype)[None], (bsz, N_META, d))
    hp = jnp.concatenate([jnp.zeros((bsz, META_PAD, d), x.dtype), meta, x], axis=1)

    lb_cum = jnp.cumsum(jax.nn.softmax(hgrn_lb_logits.astype(F32), axis=0), axis=0)
    lower = lb_cum - lb_cum[0]
    log_lb = jnp.log(lower)
    pad_cols = D_IN_PACKED - O_DT - B_HEADS
    row2 = lambda v: v.reshape(1, -1).astype(F32)

    for l in range(depth):
        wl = w_in[l]
        w_in_p = jnp.concatenate(
            [wl[:, :2816], wl[:, 2822:3078], wl[:, 2816:2822], jnp.zeros((d, pad_cols), wl.dtype)], axis=1).astype(BF16)
        b_bd, c_bd, pow_re, pow_im = _s5_params(s5_a_re[l], s5_a_im[l], s5_log_dt[l], s5_b_re[l], s5_b_im[l],
                                                s5_c_re[l], s5_c_im[l])
        pad6 = lambda v: jnp.concatenate([v.astype(F32), jnp.zeros((LANE - B_HEADS,), F32)]).reshape(1, LANE)
        params = [
            w_in_p, row2(log_lb[l]), row2(1.0 - lower[l]), row2(jnp.tile(hgrn_norm_w[l], A_HEADS)),
            m2_conv_w[l].astype(F32), row2(m2_conv_b[l]), pad6(m2_dt_bias[l]), pad6(m2_a_log[l]),
            row2(jnp.repeat(m2_d[l], B_P)), row2(m2_norm_w[l]),
            b_bd, c_bd, pow_re, pow_im, row2(s5_d[l]), s5_glu_w[l].astype(BF16), row2(s5_glu_b[l]),
            w_out[l].astype(BF16), row2(ln1_g[l]), row2(ln1_b[l]),
        ]
        hp = _mixer_call(hp, params, alpha)
        h2 = _mlp_call(hp.reshape(bsz * lp, d), w_mlp_in[l].astype(BF16), w_mlp_out[l].astype(BF16),
                       row2(ln2_g[l]), row2(ln2_b[l]), alpha)
        hp = h2.reshape(bsz, lp, d)
    return hp[:, META_PAD + N_META:]
```

```python
import functools

import jax
import jax.numpy as jnp
from jax import lax
from jax.experimental import pallas as pl
from jax.experimental.pallas import tpu as pltpu

F32 = jnp.float32
BF16 = jnp.bfloat16

D_MODEL = 1024
N_META = 16
A_HEADS, A_DK, A_W = 6, 64, 384
B_HEADS, B_P, B_W, B_G, B_N = 6, 64, 384, 2, 128
B_CONV = 4
C_G, C_CH, C_W, C_N = 16, 16, 256, 64
C_S = C_G * C_N
D_FF = 4 * D_MODEL
LN_EPS = 1e-5
RMS_EPS = 1e-6
S5_MAX_RE = -1e-4

O_Q, O_F, O_I, O_G, O_Z, O_XBC, O_U, O_DT = 0, 384, 768, 1152, 1536, 1920, 2816, 3072
D_IN_PACKED = 3200
XBC_W = 896

ROW_BLOCK = 256
MLP_ROW_BLOCK = 512
LANE = 128
SUBLANE = 8
VMEM_LIMIT = 56 * 1024 * 1024


def _sigmoid(x):
    return 0.5 * jnp.tanh(0.5 * x) + 0.5


def _silu(x):
    return x * _sigmoid(x)


def _log1p_exp_neg_abs(x):
    return jnp.log(1.0 + jnp.exp(-jnp.abs(x)))


def _softplus(x):
    return jnp.maximum(x, 0.0) + _log1p_exp_neg_abs(x)


def _split_bf16(x, parts):
    out = []
    r = x
    for i in range(parts):
        p = r.astype(BF16)
        out.append(p)
        if i + 1 < parts:
            r = r - p.astype(F32)
    return out


def _dot(a, b):
    return jnp.dot(a, b, preferred_element_type=F32)


def _dot_nt(a, b):
    return lax.dot_general(a, b, (((1,), (1,)), ((), ())), preferred_element_type=F32)


def _dot_tn(a, b):
    return lax.dot_general(a, b, (((0,), (0,)), ((), ())), preferred_element_type=F32)


def _shift_rows(x, n):
    rows = x.shape[0]
    if n == 0:
        return x
    if n % SUBLANE == 0:
        z = jnp.zeros((abs(n), x.shape[1]), x.dtype)
        if n > 0:
            return jnp.concatenate([z, x[: rows - n]], axis=0)
        return jnp.concatenate([x[-n:], z], axis=0)
    return pltpu.roll(x, n % rows, axis=0)


def _layer_norm(x, g, b):
    mu = jnp.mean(x, axis=-1, keepdims=True)
    xc = x - mu
    var = jnp.mean(xc * xc, axis=-1, keepdims=True)
    return xc * lax.rsqrt(var + LN_EPS) * g + b


def _hgrn2_pair(proj, pair, log_lb, one_m_lb, norm_w, pad_row, st_ref):
    sl = slice(LANE * pair, LANE * (pair + 1))
    col = lambda off: proj(off + LANE * pair, off + LANE * (pair + 1))
    f_raw = col(O_F)
    q_raw = col(O_Q)
    rows = f_raw.shape[0]
    assert rows == 256
    yield
    q = _silu(q_raw)
    lse = _log1p_exp_neg_abs(f_raw)
    ls_pos = -(jnp.maximum(-f_raw, 0.0) + lse)
    ls_neg = -(jnp.maximum(f_raw, 0.0) + lse)
    b_term = log_lb[:, sl] + ls_neg
    lf = jnp.maximum(ls_pos, b_term) + _log1p_exp_neg_abs(ls_pos - b_term)
    lf = jnp.where(pad_row, 0.0, lf)
    k = jnp.where(pad_row, 0.0, one_m_lb[:, sl] * jnp.exp(ls_neg))
    yield

    row = lax.broadcasted_iota(jnp.int32, (rows, 1), 0)
    lane = lax.broadcasted_iota(jnp.int32, (1, LANE), 1)
    lo = lane < 64
    zero = jnp.zeros_like(q)

    def pack_heads(s1, s2, s3):
        s1r = pltpu.roll(s1, 64, axis=1)
        s2r = pltpu.roll(s2, 64, axis=1)
        even = jnp.concatenate([jnp.where(lo, s1, s2r), jnp.where(lo, s3, zero)], axis=1)
        odd = jnp.concatenate([jnp.where(lo, s1r, s2), jnp.where(lo, zero, s3)], axis=1)
        return even.astype(BF16), odd.astype(BF16)

    cs, sf, tot = lf, jnp.zeros_like(lf), lf
    scores = {}
    for c in (1, 4, 16, 64):
        j = (row // c) % 4
        qt = q * jnp.exp(cs)
        up1, up2 = _shift_rows(tot, -c), _shift_rows(tot, -2 * c)
        e2 = sf + up1
        e3 = e2 + up2
        ke1, ke2, ke3 = k * jnp.exp(sf), k * jnp.exp(e2), k * jnp.exp(e3)
        qa, qb = pack_heads(*[jnp.where(j == i, qt, zero) for i in (1, 2, 3)])
        ka, kb = pack_heads(jnp.where(j == 0, ke1, zero),
                            jnp.where(j == 1, ke1, jnp.where(j == 0, ke2, zero)),
                            jnp.where(j == 2, ke1, jnp.where(j == 1, ke2, jnp.where(j == 0, ke3, zero))))
        scores[c] = (_dot_nt(qa, ka), _dot_nt(qb, kb))
        yield
        new_cs = cs + (jnp.where(j >= 1, _shift_rows(tot, c), 0.0) + jnp.where(j >= 2, _shift_rows(tot, 2 * c), 0.0)
                       + jnp.where(j >= 3, _shift_rows(tot, 3 * c), 0.0))
        new_sf = sf + (jnp.where(j <= 2, up1, 0.0) + jnp.where(j <= 1, up2, 0.0)
                       + jnp.where(j <= 0, _shift_rows(tot, -3 * c), 0.0))
        cs, sf = new_cs, new_sf
        tot = cs + sf
        yield

    q_in = (q * jnp.exp(cs)).astype(BF16)
    k_out = (k * jnp.exp(sf)).astype(BF16)
    decay_blk = jnp.exp(tot[0:1, :])
    qp, kp = q.astype(BF16), k.astype(BF16)
    zero_b = jnp.zeros_like(qp)
    s_d = (_dot_nt(jnp.where(lo, qp, zero_b), kp), _dot_nt(jnp.where(lo, zero_b, qp), kp))
    v_raw = col(O_I)
    vp = v_raw.astype(BF16)
    yield

    tt = lax.broadcasted_iota(jnp.int32, (rows, rows), 0)
    ss = lax.broadcasted_iota(jnp.int32, (rows, rows), 1)
    same64 = (tt // 64) == (ss // 64)
    same16 = (tt // 16) == (ss // 16)
    same4 = (tt // 4) == (ss // 4)
    diag = tt == ss
    sc = []
    for hh in range(2):
        sc.append(jnp.where(same4, jnp.where(diag, s_d[hh], scores[1][hh]),
                            jnp.where(same16, scores[4][hh],
                                      jnp.where(same64, scores[16][hh], scores[64][hh]))).astype(BF16))
        yield
    st = st_ref[pair]
    vv = jnp.concatenate([jnp.where(lo, vp, zero_b), jnp.where(lo, zero_b, vp)], axis=0)
    o = _dot(jnp.concatenate(sc, axis=1), vv) + _dot_nt(q_in, st.astype(BF16))
    upd = _dot_tn(vp, k_out)
    r2 = lax.broadcasted_iota(jnp.int32, (LANE, LANE), 0)
    c2 = lax.broadcasted_iota(jnp.int32, (LANE, LANE), 1)
    head_diag = (r2 // 64) == (c2 // 64)
    st_ref[pair] = st * decay_blk + jnp.where(head_diag, upd, 0.0)
    g_raw = col(O_G)
    yield

    head_ones = jnp.where(head_diag, 1.0, 0.0).astype(BF16)
    ms = _dot((o * o).astype(BF16), head_ones) * (1.0 / A_DK)
    return o * lax.rsqrt(ms + RMS_EPS) * norm_w[:, sl] * _silu(g_raw)


def _ssd(proj, conv_w, conv_b, dt_bias, a_log, d_full, norm_w, pad_row, hist_ref, st_ref):
    xbc_raw = proj(O_XBC, O_U)
    dt_raw = proj(O_DT, D_IN_PACKED)
    rows = xbc_raw.shape[0]
    yield
    cat = jnp.concatenate([hist_ref[...], xbc_raw], axis=0)
    hist_ref[...] = xbc_raw[rows - SUBLANE:, :]
    acc = conv_b + conv_w[B_CONV - 1:B_CONV, :] * xbc_raw
    for n in (1, 2, 3):
        shifted = pltpu.roll(cat, n, axis=0)[SUBLANE:, :]
        acc = acc + conv_w[B_CONV - 1 - n:B_CONV - n, :] * shifted
    yield
    xbc = _silu(acc)
    xs = xbc[:, :B_W]
    bm = xbc[:, B_W:B_W + B_G * B_N].astype(BF16)
    cm = xbc[:, B_W + B_G * B_N:].astype(BF16)

    lane = lax.broadcasted_iota(jnp.int32, (1, LANE), 1)
    dt = jnp.where(pad_row, 0.0, _softplus(dt_raw + dt_bias))
    a_neg = jnp.where(lane < B_HEADS, -jnp.exp(a_log), 0.0)
    d_a = dt * a_neg
    yield

    rr = lax.broadcasted_iota(jnp.int32, (rows, rows), 0)
    cc = lax.broadcasted_iota(jnp.int32, (rows, rows), 1)
    causal = cc <= rr
    tril = jnp.where(causal, 1.0, 0.0).astype(BF16)
    triu = jnp.where(rr <= cc, 1.0, 0.0).astype(BF16)
    d_a_parts = _split_bf16(d_a, 3)
    cum = sum(_dot(tril, part) for part in d_a_parts)
    cum_t = sum(_dot_tn(part, triu) for part in d_a_parts)
    yield

    er = lax.broadcasted_iota(jnp.int32, (LANE, B_W), 0)
    ec = lax.broadcasted_iota(jnp.int32, (LANE, B_W), 1)
    expand = jnp.where(er == ec // B_P, 1.0, 0.0).astype(BF16)
    dt_full = sum(_dot(part, expand) for part in _split_bf16(dt, 3))
    cum_full = sum(_dot(part, expand) for part in _split_bf16(cum, 3))
    xdt = xs * dt_full
    lo = lane < 64
    cb = [_dot_nt(cm[:, B_N * g:B_N * (g + 1)], bm[:, B_N * g:B_N * (g + 1)]) for g in range(B_G)]
    yield

    y_parts = []
    for p in range(B_HEADS // 2):
        sc = []
        for h in (2 * p, 2 * p + 1):
            diff = cum[:, h:h + 1] - cum_t[h:h + 1, :]
            decay = jnp.exp(jnp.where(causal, diff, -jnp.inf))
            sc.append((cb[h // (B_HEADS // B_G)] * decay).astype(BF16))
        xp = xdt[:, LANE * p:LANE * (p + 1)]
        zero = jnp.zeros_like(xp)
        xx = jnp.concatenate([jnp.where(lo, xp, zero), jnp.where(lo, zero, xp)], axis=0).astype(BF16)
        y_parts.append(_dot(jnp.concatenate(sc, axis=1), xx))
        yield
    y = jnp.concatenate(y_parts, axis=1)

    st = st_ref[...]
    y = y + _dot(cm, st.astype(BF16)) * jnp.exp(cum_full) + d_full * xs
    cum_last = cum_full[rows - 1:rows, :]
    upd = _dot_tn(bm, (xdt * jnp.exp(cum_last - cum_full)).astype(BF16))
    sr = lax.broadcasted_iota(jnp.int32, (B_G * B_N, B_W), 0)
    sc = lax.broadcasted_iota(jnp.int32, (B_G * B_N, B_W), 1)
    st_ref[...] = st * jnp.exp(cum_last) + jnp.where((sr // B_N) == (sc // (B_W // B_G)), upd, 0.0)
    z_raw = proj(O_Z, O_XBC)
    yield

    y = y * _silu(z_raw)
    gr = lax.broadcasted_iota(jnp.int32, (B_W, B_W), 0)
    gc = lax.broadcasted_iota(jnp.int32, (B_W, B_W), 1)
    group_ones = jnp.where((gr // (B_W // B_G)) == (gc // (B_W // B_G)), 1.0, 0.0).astype(BF16)
    ms = _dot((y * y).astype(BF16), group_ones) * (1.0 / (B_W // B_G))
    return y * lax.rsqrt(ms + RMS_EPS) * norm_w


def _s5(proj, b_bd, c_bd, pow_re, pow_im, d_skip, glu_w, glu_b, carry_ref):
    u = proj(O_U, O_DT)
    rows = u.shape[0]
    groups = rows // SUBLANE
    bu = _dot(u.astype(BF16), b_bd)
    yield
    x_re = bu[:, :C_S].reshape(groups, SUBLANE, C_S)
    x_im = bu[:, C_S:].reshape(groups, SUBLANE, C_S)
    sub = lax.broadcasted_iota(jnp.int32, (SUBLANE, 1), 0)
    for d in (1, 2, 4):
        keep = sub >= d
        p_re = jnp.where(keep, pow_re[d - 1:d, :], 0.0)[None]
        p_im = jnp.where(keep, pow_im[d - 1:d, :], 0.0)[None]
        s_re = pltpu.roll(x_re, d, axis=1)
        s_im = pltpu.roll(x_im, d, axis=1)
        x_re, x_im = (x_re + (p_re * s_re - p_im * s_im), x_im + (p_re * s_im + p_im * s_re))
        yield

    c_re, c_im = carry_ref[0], carry_ref[1]
    g_re, g_im = [], []
    for i in range(groups):
        g_re.append(x_re[i] + (pow_re * c_re - pow_im * c_im))
        g_im.append(x_im[i] + (pow_re * c_im + pow_im * c_re))
        c_re, c_im = g_re[-1][SUBLANE - 1:, :], g_im[-1][SUBLANE - 1:, :]
        if i % 4 == 3:
            yield
    carry_ref[0] = c_re
    carry_ref[1] = c_im

    y = (_dot(jnp.concatenate(g_re, axis=0).astype(BF16), c_bd[:C_S, :])
         + _dot(jnp.concatenate(g_im, axis=0).astype(BF16), c_bd[C_S:, :]) + d_skip * u)
    yield
    y = jax.nn.gelu(y, approximate=True)
    gate = _dot(y.astype(BF16), glu_w) + glu_b
    yield
    return y * _sigmoid(gate)


def _interleave(gens):
    results = [None] * len(gens)
    live = list(range(len(gens)))
    while live:
        for i in list(live):
            try:
                next(gens[i])
            except StopIteration as stop:
                results[i] = stop.value
                live.remove(i)
    return results


def _mixer_kernel(alpha, front_pad, h_ref, w_in_ref, log_lb_ref, one_m_lb_ref, a_norm_ref,
                  conv_w_ref, conv_b_ref, dt_bias_ref, a_log_ref, d_full_ref, b_norm_ref,
                  s5_b_ref, s5_c_ref, s5_pre_ref, s5_pim_ref, s5_d_ref, glu_w_ref, glu_b_ref,
                  w_out_ref, ln_g_ref, ln_b_ref, o_ref,
                  a_state, b_hist, b_state, c_carry):
    blk = pl.program_id(1)
    rows = h_ref.shape[1]

    @pl.when(blk == 0)
    def _():
        a_state[...] = jnp.zeros_like(a_state)
        b_hist[...] = jnp.zeros_like(b_hist)
        b_state[...] = jnp.zeros_like(b_state)
        c_carry[...] = jnp.zeros_like(c_carry)

    row = lax.broadcasted_iota(jnp.int32, (rows, 1), 0) + blk * rows
    pad_row = row < front_pad
    h = h_ref[0]
    hb = jnp.where(pad_row, 0.0, h).astype(BF16)

    def proj(lo_col, hi_col):
        return _dot(hb, w_in_ref[:, lo_col:hi_col])

    gens = [_s5(proj, s5_b_ref[...], s5_c_ref[...], s5_pre_ref[...], s5_pim_ref[...],
                s5_d_ref[...], glu_w_ref[...], glu_b_ref[...], c_carry),
            _ssd(proj, conv_w_ref[...], conv_b_ref[...], dt_bias_ref[...], a_log_ref[...],
                 d_full_ref[...], b_norm_ref[...], pad_row, b_hist, b_state)]
    gens += [_hgrn2_pair(proj, p, log_lb_ref[...], one_m_lb_ref[...], a_norm_ref[...], pad_row, a_state)
             for p in range(A_HEADS // 2)]
    y_c, y_b, *y_a = _interleave(gens)
    mixed = _dot(y_c.astype(BF16), w_out_ref[A_W + B_W:, :]) + _dot(y_b.astype(BF16), w_out_ref[A_W:A_W + B_W, :])
    for p, y in enumerate(y_a):
        mixed = mixed + _dot(y.astype(BF16), w_out_ref[LANE * p:LANE * (p + 1), :])
    o_ref[0] = _layer_norm(alpha * h + mixed, ln_g_ref[...], ln_b_ref[...])


def _layer_spec(arr, layer, n_grid):
    if n_grid == 2:
        return pl.BlockSpec((None,) + arr.shape[1:], lambda b, j: (layer, 0, 0))
    return pl.BlockSpec((None,) + arr.shape[1:], lambda i: (layer, 0, 0))


def _mixer_call(hp, params, layer, alpha, front_pad):
    bsz, lp, d = hp.shape
    rows = ROW_BLOCK
    assert lp % rows == 0
    h_spec = pl.BlockSpec((1, rows, d), lambda b, j: (b, j, 0))
    return pl.pallas_call(
        functools.partial(_mixer_kernel, alpha, front_pad),
        grid=(bsz, lp // rows),
        in_specs=[h_spec] + [_layer_spec(p, layer, 2) for p in params],
        out_specs=h_spec,
        out_shape=jax.ShapeDtypeStruct(hp.shape, F32),
        scratch_shapes=[
            pltpu.VMEM((A_HEADS // 2, LANE, LANE), F32),
            pltpu.VMEM((SUBLANE, XBC_W), F32),
            pltpu.VMEM((B_G * B_N, B_W), F32),
            pltpu.VMEM((2, 1, C_S), F32),
        ],
        compiler_params=pltpu.CompilerParams(
            dimension_semantics=("arbitrary", "arbitrary"), vmem_limit_bytes=VMEM_LIMIT),
        name="mixer",
    )(hp, *params)


def _mlp_kernel(alpha, h_ref, w1_ref, w2_ref, ln_g_ref, ln_b_ref, o_ref):
    h = h_ref[...]
    hb = h.astype(BF16)
    ff = jnp.zeros_like(h)
    step = D_MODEL
    for j in range(D_FF // step):
        hid = jnp.maximum(_dot(hb, w1_ref[:, j * step:(j + 1) * step]), 0.0)
        ff = ff + _dot((hid * hid).astype(BF16), w2_ref[j * step:(j + 1) * step, :])
    o_ref[...] = _layer_norm(alpha * h + ff, ln_g_ref[...], ln_b_ref[...])


def _mlp_call(h2, params, layer, alpha):
    n, d = h2.shape
    rows = MLP_ROW_BLOCK
    assert n % rows == 0
    h_spec = pl.BlockSpec((rows, d), lambda i: (i, 0))
    return pl.pallas_call(
        functools.partial(_mlp_kernel, alpha),
        grid=(n // rows,),
        in_specs=[h_spec] + [_layer_spec(p, layer, 1) for p in params],
        out_specs=h_spec,
        out_shape=jax.ShapeDtypeStruct(h2.shape, F32),
        compiler_params=pltpu.CompilerParams(
            dimension_semantics=("arbitrary",), vmem_limit_bytes=VMEM_LIMIT),
        name="mlp",
    )(h2, *params)


def _block_diag(blocks):
    g, r, c = blocks.shape
    tiled = jnp.tile(blocks.reshape(g * r, c), (1, g))
    rg = lax.broadcasted_iota(jnp.int32, (g * r, g * c), 0) // r
    cg = lax.broadcasted_iota(jnp.int32, (g * r, g * c), 1) // c
    return jnp.where(rg == cg, tiled, 0.0)


def _s5_params(a_re, a_im, log_dt, b_re, b_im, c_re, c_im):
    lam_re = jnp.minimum(a_re, S5_MAX_RE)
    lam_im = a_im
    dt = jnp.exp(log_dt)[:, None]
    mag = jnp.exp(lam_re * dt)
    lb_re = mag * jnp.cos(lam_im * dt)
    lb_im = mag * jnp.sin(lam_im * dt)
    den = jnp.square(lam_re) + jnp.square(lam_im)
    nr = lb_re - 1.0
    s_re = (nr * lam_re + lb_im * lam_im) / den
    s_im = (lb_im * lam_re - nr * lam_im) / den
    bb_re = s_re[..., None] * b_re - s_im[..., None] * b_im
    bb_im = s_re[..., None] * b_im + s_im[..., None] * b_re
    b_bd = jnp.concatenate([_block_diag(bb_re.transpose(0, 2, 1)), _block_diag(bb_im.transpose(0, 2, 1))], axis=1)
    c_bd = jnp.concatenate([_block_diag(c_re.transpose(0, 2, 1)), -_block_diag(c_im.transpose(0, 2, 1))], axis=0)
    steps = jnp.arange(1, SUBLANE + 1, dtype=F32)[:, None, None]
    pmag = jnp.exp(lam_re * dt * steps)
    pow_re = (pmag * jnp.cos(lam_im * dt * steps)).reshape(SUBLANE, C_S)
    pow_im = (pmag * jnp.sin(lam_im * dt * steps)).reshape(SUBLANE, C_S)
    return b_bd.astype(BF16), c_bd.astype(BF16), pow_re, pow_im


def kernel(x, meta_tokens, w_in, hgrn_lb_logits, hgrn_norm_w, m2_conv_w, m2_conv_b, m2_dt_bias, m2_a_log, m2_d, m2_norm_w, s5_a_re, s5_a_im, s5_log_dt, s5_b_re, s5_b_im, s5_c_re, s5_c_im, s5_d, s5_glu_w, s5_glu_b, w_out, ln1_g, ln1_b, w_mlp_in, w_mlp_out, ln2_g, ln2_b):
    bsz, seq, d = x.shape
    depth = w_in.shape[0]
    alpha = (2 * depth) ** 0.25
    lp = -(-(N_META + seq) // ROW_BLOCK) * ROW_BLOCK
    front_pad = lp - N_META - seq

    meta = jnp.broadcast_to(meta_tokens.astype(x.dtype)[None], (bsz, N_META, d))
    hp = jnp.concatenate([jnp.zeros((bsz, front_pad, d), x.dtype), meta, x], axis=1)

    f32 = lambda v: v.astype(F32)
    row3 = lambda v: f32(v).reshape(depth, 1, -1)
    lb_cum = jnp.cumsum(jax.nn.softmax(f32(hgrn_lb_logits), axis=0), axis=0)
    lower = lb_cum - lb_cum[0]
    pad_cols = D_IN_PACKED - O_DT - B_HEADS
    w_in_p = jnp.concatenate([w_in[:, :, :2816], w_in[:, :, 2822:3078], w_in[:, :, 2816:2822],
                              jnp.zeros((depth, d, pad_cols), w_in.dtype)], axis=2).astype(BF16)
    b_bd, c_bd, pow_re, pow_im = jax.vmap(_s5_params)(
        f32(s5_a_re), f32(s5_a_im), f32(s5_log_dt), f32(s5_b_re), f32(s5_b_im), f32(s5_c_re), f32(s5_c_im))
    pad6 = lambda v: jnp.concatenate([f32(v), jnp.zeros((depth, LANE - B_HEADS), F32)], axis=1).reshape(depth, 1, LANE)
    mixer_params = [
        w_in_p, row3(jnp.log(lower)), row3(1.0 - lower), row3(jnp.tile(hgrn_norm_w, (1, A_HEADS))),
        f32(m2_conv_w), row3(m2_conv_b), pad6(m2_dt_bias), pad6(m2_a_log),
        row3(jnp.repeat(m2_d, B_P, axis=1)), row3(m2_norm_w),
        b_bd, c_bd, pow_re, pow_im, row3(s5_d), s5_glu_w.astype(BF16), row3(s5_glu_b),
        w_out.astype(BF16), row3(ln1_g), row3(ln1_b),
    ]
    mlp_params = [w_mlp_in.astype(BF16), w_mlp_out.astype(BF16), row3(ln2_g), row3(ln2_b)]

    for l in range(depth):
        hp = _mixer_call(hp, mixer_params, l, alpha, front_pad)
        hp = _mlp_call(hp.reshape(bsz * lp, d), mlp_params, l, alpha).reshape(bsz, lp, d)
    return hp[:, front_pad + N_META:]
```

```python
import functools

import jax
import jax.numpy as jnp
from jax import lax
from jax.experimental import pallas as pl
from jax.experimental.pallas import tpu as pltpu

F32 = jnp.float32
BF16 = jnp.bfloat16

D_MODEL = 1024
N_META = 16
A_HEADS, A_DK, A_W = 6, 64, 384
B_HEADS, B_P, B_W, B_G, B_N = 6, 64, 384, 2, 128
B_CONV = 4
C_G, C_CH, C_W, C_N = 16, 16, 256, 64
C_S = C_G * C_N
D_FF = 4 * D_MODEL
LN_EPS = 1e-5
RMS_EPS = 1e-6
S5_MAX_RE = -1e-4
LOG2_E = 1.4426950408889634

O_Q, O_F, O_I, O_G, O_Z, O_XBC, O_U, O_DT = 0, 384, 768, 1152, 1536, 1920, 2816, 3072
D_IN_PACKED = 3200
XBC_W = 896

ROW_BLOCK = 256
MLP_ROW_BLOCK = 512
LANE = 128
SUBLANE = 8
VMEM_LIMIT = 56 * 1024 * 1024


def _sigmoid(x):
    return 0.5 * jnp.tanh(0.5 * x) + 0.5


def _silu(x):
    hx = 0.5 * x
    return hx + hx * jnp.tanh(hx)


def _log1p_exp_neg_abs(x):
    return jnp.log(1.0 + jnp.exp(-jnp.abs(x)))


def _softplus(x):
    return jnp.maximum(x, 0.0) + _log1p_exp_neg_abs(x)


def _split_bf16(x, parts):
    out = []
    r = x
    for i in range(parts):
        p = r.astype(BF16)
        out.append(p)
        if i + 1 < parts:
            r = r - p.astype(F32)
    return out


def _dot(a, b):
    return jnp.dot(a, b, preferred_element_type=F32)


def _dot_nt(a, b):
    return lax.dot_general(a, b, (((1,), (1,)), ((), ())), preferred_element_type=F32)


def _dot_tn(a, b):
    return lax.dot_general(a, b, (((0,), (0,)), ((), ())), preferred_element_type=F32)


def _shift_rows(x, n):
    rows = x.shape[0]
    if n == 0:
        return x
    if n % SUBLANE == 0:
        z = jnp.zeros((abs(n), x.shape[1]), x.dtype)
        if n > 0:
            return jnp.concatenate([z, x[: rows - n]], axis=0)
        return jnp.concatenate([x[-n:], z], axis=0)
    return pltpu.roll(x, n % rows, axis=0)


def _layer_norm(x, g, b):
    mu = jnp.mean(x, axis=-1, keepdims=True)
    xc = x - mu
    var = jnp.mean(xc * xc, axis=-1, keepdims=True)
    return xc * lax.rsqrt(var + LN_EPS) * g + b


def _hgrn2_pair(proj, pair, log_lb, one_m_lb, norm_w, pad_row, group_masks, st_ref):
    sl = slice(LANE * pair, LANE * (pair + 1))
    col = lambda off: proj(off + LANE * pair, off + LANE * (pair + 1))
    f_raw = col(O_F)
    q_raw = col(O_Q)
    rows = f_raw.shape[0]
    assert rows == 256
    yield
    q = _silu(q_raw)
    lse = _log1p_exp_neg_abs(f_raw)
    ls_pos = -(jnp.maximum(-f_raw, 0.0) + lse)
    ls_neg = -(jnp.maximum(f_raw, 0.0) + lse)
    b_term = log_lb[:, sl] + ls_neg
    lf = jnp.maximum(ls_pos, b_term) + _log1p_exp_neg_abs(ls_pos - b_term)
    lf = jnp.where(pad_row, 0.0, lf)
    k = jnp.where(pad_row, 0.0, one_m_lb[:, sl] * jnp.exp(ls_neg))
    yield

    row = lax.broadcasted_iota(jnp.int32, (rows, 1), 0)
    lane = lax.broadcasted_iota(jnp.int32, (1, LANE), 1)
    lo = lane < 64
    zero = jnp.zeros_like(q)

    def pack_heads(s1, s2, s3):
        s1r = pltpu.roll(s1, 64, axis=1)
        s2r = pltpu.roll(s2, 64, axis=1)
        even = jnp.concatenate([jnp.where(lo, s1, s2r), jnp.where(lo, s3, zero)], axis=1)
        odd = jnp.concatenate([jnp.where(lo, s1r, s2), jnp.where(lo, zero, s3)], axis=1)
        return even.astype(BF16), odd.astype(BF16)

    lf = lf * LOG2_E
    cs, sf, tot = lf, jnp.zeros_like(lf), lf
    scores = {}
    for c in (1, 4, 16, 64):
        j = (row // c) % 4
        qt = q * jnp.exp2(cs)
        up1, up2 = _shift_rows(tot, -c), _shift_rows(tot, -2 * c)
        e2 = sf + up1
        e3 = e2 + up2
        ke1, ke2, ke3 = k * jnp.exp2(sf), k * jnp.exp2(e2), k * jnp.exp2(e3)
        qa, qb = pack_heads(*[jnp.where(j == i, qt, zero) for i in (1, 2, 3)])
        ka, kb = pack_heads(jnp.where(j == 0, ke1, zero),
                            jnp.where(j == 1, ke1, jnp.where(j == 0, ke2, zero)),
                            jnp.where(j == 2, ke1, jnp.where(j == 1, ke2, jnp.where(j == 0, ke3, zero))))
        scores[c] = (_dot_nt(qa, ka), _dot_nt(qb, kb))
        yield
        new_cs = cs + (jnp.where(j >= 1, _shift_rows(tot, c), 0.0) + jnp.where(j >= 2, _shift_rows(tot, 2 * c), 0.0)
                       + jnp.where(j >= 3, _shift_rows(tot, 3 * c), 0.0))
        new_sf = sf + (jnp.where(j <= 2, up1, 0.0) + jnp.where(j <= 1, up2, 0.0)
                       + jnp.where(j <= 0, _shift_rows(tot, -3 * c), 0.0))
        cs, sf = new_cs, new_sf
        tot = cs + sf
        yield

    q_in = (q * jnp.exp2(cs)).astype(BF16)
    k_out = (k * jnp.exp2(sf)).astype(BF16)
    decay_blk = jnp.exp2(tot[0:1, :])
    qp, kp = q.astype(BF16), k.astype(BF16)
    zero_b = jnp.zeros_like(qp)
    s_d = (_dot_nt(jnp.where(lo, qp, zero_b), kp), _dot_nt(jnp.where(lo, zero_b, qp), kp))
    v_raw = col(O_I)
    vp = v_raw.astype(BF16)
    yield

    same64, same16, same4, diag = group_masks
    sc = []
    for hh in range(2):
        sc.append(jnp.where(same4, jnp.where(diag, s_d[hh], scores[1][hh]),
                            jnp.where(same16, scores[4][hh],
                                      jnp.where(same64, scores[16][hh], scores[64][hh]))).astype(BF16))
        yield
    st = st_ref[pair]
    vv = jnp.concatenate([jnp.where(lo, vp, zero_b), jnp.where(lo, zero_b, vp)], axis=0)
    o = _dot(jnp.concatenate(sc, axis=1), vv) + _dot_nt(q_in, st.astype(BF16))
    upd = _dot_tn(vp, k_out)
    r2 = lax.broadcasted_iota(jnp.int32, (LANE, LANE), 0)
    c2 = lax.broadcasted_iota(jnp.int32, (LANE, LANE), 1)
    head_diag = (r2 // 64) == (c2 // 64)
    st_ref[pair] = st * decay_blk + jnp.where(head_diag, upd, 0.0)
    g_raw = col(O_G)
    yield

    head_ones = jnp.where(head_diag, 1.0, 0.0).astype(BF16)
    ms = _dot((o * o).astype(BF16), head_ones) * (1.0 / A_DK)
    return o * lax.rsqrt(ms + RMS_EPS) * norm_w[:, sl] * _silu(g_raw)


def _ssd(proj, conv_w, conv_b, dt_bias, a_log, d_full, norm_w, pad_row, hist_ref, st_ref):
    xbc_raw = proj(O_XBC, O_U)
    dt_raw = proj(O_DT, D_IN_PACKED)
    rows = xbc_raw.shape[0]
    yield
    cat = jnp.concatenate([hist_ref[...], xbc_raw], axis=0)
    hist_ref[...] = xbc_raw[rows - SUBLANE:, :]
    acc = conv_b + conv_w[B_CONV - 1:B_CONV, :] * xbc_raw
    for n in (1, 2, 3):
        shifted = pltpu.roll(cat, n, axis=0)[SUBLANE:, :]
        acc = acc + conv_w[B_CONV - 1 - n:B_CONV - n, :] * shifted
    yield
    xbc = _silu(acc)
    xs = xbc[:, :B_W]
    bm = xbc[:, B_W:B_W + B_G * B_N].astype(BF16)
    cm = xbc[:, B_W + B_G * B_N:].astype(BF16)

    lane = lax.broadcasted_iota(jnp.int32, (1, LANE), 1)
    dt = jnp.where(pad_row, 0.0, _softplus(dt_raw + dt_bias))
    a_neg = jnp.where(lane < B_HEADS, -jnp.exp(a_log), 0.0)
    d_a = dt * (a_neg * LOG2_E)
    yield

    rr = lax.broadcasted_iota(jnp.int32, (rows, rows), 0)
    cc = lax.broadcasted_iota(jnp.int32, (rows, rows), 1)
    causal = cc <= rr
    tril = jnp.where(causal, 1.0, 0.0).astype(BF16)
    triu = jnp.where(rr <= cc, 1.0, 0.0).astype(BF16)
    d_a_parts = jnp.concatenate(_split_bf16(d_a, 3), axis=0)
    cum = _dot(jnp.concatenate([tril] * 3, axis=1), d_a_parts)
    cum_t = _dot_tn(d_a_parts, jnp.concatenate([triu] * 3, axis=0))
    yield

    er = lax.broadcasted_iota(jnp.int32, (3 * LANE, B_W), 0)
    ec = lax.broadcasted_iota(jnp.int32, (3 * LANE, B_W), 1)
    expand = jnp.where(er % LANE == ec // B_P, 1.0, 0.0).astype(BF16)
    dt_full = _dot(jnp.concatenate(_split_bf16(dt, 3), axis=1), expand)
    cum_full = _dot(jnp.concatenate(_split_bf16(cum, 3), axis=1), expand)
    xdt = xs * dt_full
    lo = lane < 64
    cb = [_dot_nt(cm[:, B_N * g:B_N * (g + 1)], bm[:, B_N * g:B_N * (g + 1)]) for g in range(B_G)]
    yield

    y_parts = []
    for p in range(B_HEADS // 2):
        sc = []
        for h in (2 * p, 2 * p + 1):
            diff = cum[:, h:h + 1] - cum_t[h:h + 1, :]
            decay = jnp.exp2(jnp.where(causal, diff, -jnp.inf))
            sc.append((cb[h // (B_HEADS // B_G)] * decay).astype(BF16))
        xp = xdt[:, LANE * p:LANE * (p + 1)]
        zero = jnp.zeros_like(xp)
        xx = jnp.concatenate([jnp.where(lo, xp, zero), jnp.where(lo, zero, xp)], axis=0).astype(BF16)
        y_parts.append(_dot(jnp.concatenate(sc, axis=1), xx))
        yield
    y = jnp.concatenate(y_parts, axis=1)

    st = st_ref[...]
    y = y + _dot(cm, st.astype(BF16)) * jnp.exp2(cum_full) + d_full * xs
    cum_last = cum_full[rows - 1:rows, :]
    upd = _dot_tn(bm, (xdt * jnp.exp2(cum_last - cum_full)).astype(BF16))
    sr = lax.broadcasted_iota(jnp.int32, (B_G * B_N, B_W), 0)
    sc = lax.broadcasted_iota(jnp.int32, (B_G * B_N, B_W), 1)
    st_ref[...] = st * jnp.exp2(cum_last) + jnp.where((sr // B_N) == (sc // (B_W // B_G)), upd, 0.0)
    z_raw = proj(O_Z, O_XBC)
    yield

    y = y * _silu(z_raw)
    gr = lax.broadcasted_iota(jnp.int32, (B_W, B_W), 0)
    gc = lax.broadcasted_iota(jnp.int32, (B_W, B_W), 1)
    group_ones = jnp.where((gr // (B_W // B_G)) == (gc // (B_W // B_G)), 1.0, 0.0).astype(BF16)
    ms = _dot((y * y).astype(BF16), group_ones) * (1.0 / (B_W // B_G))
    return y * lax.rsqrt(ms + RMS_EPS) * norm_w


def _s5(proj, b_bd, c_bd, pow_re, pow_im, d_skip, glu_w, glu_b, carry_ref):
    u = proj(O_U, O_DT)
    rows = u.shape[0]
    groups = rows // SUBLANE
    sub = lax.broadcasted_iota(jnp.int32, (SUBLANE, 1), 0)
    u3 = u.reshape(groups, SUBLANE, C_W)
    delayed = [u] + [jnp.where(sub >= d, pltpu.roll(u3, d, axis=1), 0.0).reshape(rows, C_W) for d in (1, 2, 3)]
    bu = _dot(jnp.concatenate([x.astype(BF16) for x in delayed], axis=1), b_bd)
    yield
    x_re = bu[:, :C_S].reshape(groups, SUBLANE, C_S)
    x_im = bu[:, C_S:].reshape(groups, SUBLANE, C_S)
    for d in (4,):
        keep = sub >= d
        p_re = jnp.where(keep, pow_re[d - 1:d, :], 0.0)[None]
        p_im = jnp.where(keep, pow_im[d - 1:d, :], 0.0)[None]
        s_re = pltpu.roll(x_re, d, axis=1)
        s_im = pltpu.roll(x_im, d, axis=1)
        x_re, x_im = (x_re + (p_re * s_re - p_im * s_im), x_im + (p_re * s_im + p_im * s_re))
        yield

    c_re, c_im = carry_ref[0], carry_ref[1]
    g_re, g_im = [], []
    for i in range(groups):
        g_re.append(x_re[i] + (pow_re * c_re - pow_im * c_im))
        g_im.append(x_im[i] + (pow_re * c_im + pow_im * c_re))
        c_re, c_im = g_re[-1][SUBLANE - 1:, :], g_im[-1][SUBLANE - 1:, :]
        if i % 4 == 3:
            yield
    carry_ref[0] = c_re
    carry_ref[1] = c_im

    y = (_dot(jnp.concatenate(g_re, axis=0).astype(BF16), c_bd[:C_S, :])
         + _dot(jnp.concatenate(g_im, axis=0).astype(BF16), c_bd[C_S:, :]) + d_skip * u)
    yield
    y = jax.nn.gelu(y, approximate=True)
    gate = _dot(y.astype(BF16), glu_w) + glu_b
    yield
    return y * _sigmoid(gate)


def _interleave(gens):
    results = [None] * len(gens)
    live = list(range(len(gens)))
    while live:
        for i in list(live):
            try:
                next(gens[i])
            except StopIteration as stop:
                results[i] = stop.value
                live.remove(i)
    return results


def _mixer_kernel(alpha, front_pad, h_ref, w_in_ref, log_lb_ref, one_m_lb_ref, a_norm_ref,
                  conv_w_ref, conv_b_ref, dt_bias_ref, a_log_ref, d_full_ref, b_norm_ref,
                  s5_b_ref, s5_c_ref, s5_pre_ref, s5_pim_ref, s5_d_ref, glu_w_ref, glu_b_ref,
                  w_out_ref, ln_g_ref, ln_b_ref, o_ref,
                  a_state, b_hist, b_state, c_carry):
    blk = pl.program_id(1)
    rows = h_ref.shape[1]

    @pl.when(blk == 0)
    def _():
        a_state[...] = jnp.zeros_like(a_state)
        b_hist[...] = jnp.zeros_like(b_hist)
        b_state[...] = jnp.zeros_like(b_state)
        c_carry[...] = jnp.zeros_like(c_carry)

    row = lax.broadcasted_iota(jnp.int32, (rows, 1), 0) + blk * rows
    pad_row = row < front_pad
    h = h_ref[0]
    hb = jnp.where(pad_row, 0.0, h).astype(BF16)

    groups = [(O_Q, O_I), (O_U, D_IN_PACKED), (O_Z, O_U), (O_I, O_Z)]
    projected = [(lo_col, hi_col, _dot(hb, w_in_ref[:, lo_col:hi_col])) for lo_col, hi_col in groups]

    def proj(lo_col, hi_col):
        for g_lo, g_hi, arr in projected:
            if g_lo <= lo_col and hi_col <= g_hi:
                return arr[:, lo_col - g_lo:hi_col - g_lo]
        raise ValueError((lo_col, hi_col))

    gens = [_s5(proj, s5_b_ref[...], s5_c_ref[...], s5_pre_ref[...], s5_pim_ref[...],
                s5_d_ref[...], glu_w_ref[...], glu_b_ref[...], c_carry),
            _ssd(proj, conv_w_ref[...], conv_b_ref[...], dt_bias_ref[...], a_log_ref[...],
                 d_full_ref[...], b_norm_ref[...], pad_row, b_hist, b_state)]
    tx = lax.broadcasted_iota(jnp.int32, (rows, rows), 0) ^ lax.broadcasted_iota(jnp.int32, (rows, rows), 1)
    group_masks = (tx < 64, tx < 16, tx < 4, tx == 0)
    gens += [_hgrn2_pair(proj, p, log_lb_ref[...], one_m_lb_ref[...], a_norm_ref[...], pad_row, group_masks, a_state)
             for p in range(A_HEADS // 2)]
    y_c, y_b, *y_a = _interleave(gens)
    y_all = jnp.concatenate([y.astype(BF16) for y in (*y_a, y_b, y_c)], axis=1)
    o_ref[0] = _layer_norm(alpha * h + _dot(y_all, w_out_ref[...]), ln_g_ref[...], ln_b_ref[...])


def _layer_spec(arr, layer, n_grid):
    if n_grid == 2:
        return pl.BlockSpec((None,) + arr.shape[1:], lambda b, j: (layer, 0, 0))
    return pl.BlockSpec((None,) + arr.shape[1:], lambda i: (layer, 0, 0))


def _mixer_call(hp, params, layer, alpha, front_pad):
    bsz, lp, d = hp.shape
    rows = ROW_BLOCK
    assert lp % rows == 0
    h_spec = pl.BlockSpec((1, rows, d), lambda b, j: (b, j, 0))
    return pl.pallas_call(
        functools.partial(_mixer_kernel, alpha, front_pad),
        grid=(bsz, lp // rows),
        in_specs=[h_spec] + [_layer_spec(p, layer, 2) for p in params],
        out_specs=h_spec,
        out_shape=jax.ShapeDtypeStruct(hp.shape, F32),
        scratch_shapes=[
            pltpu.VMEM((A_HEADS // 2, LANE, LANE), F32),
            pltpu.VMEM((SUBLANE, XBC_W), F32),
            pltpu.VMEM((B_G * B_N, B_W), F32),
            pltpu.VMEM((2, 1, C_S), F32),
        ],
        compiler_params=pltpu.CompilerParams(
            dimension_semantics=("arbitrary", "arbitrary"), vmem_limit_bytes=VMEM_LIMIT),
        name="mixer",
    )(hp, *params)


def _mlp_kernel(alpha, h_ref, w1_ref, w2_ref, ln_g_ref, ln_b_ref, o_ref):
    h = h_ref[...]
    hb = h.astype(BF16)
    ff = jnp.zeros_like(h)
    step = D_MODEL
    for j in range(D_FF // step):
        hid = jnp.maximum(_dot(hb, w1_ref[:, j * step:(j + 1) * step]), 0.0)
        ff = ff + _dot((hid * hid).astype(BF16), w2_ref[j * step:(j + 1) * step, :])
    o_ref[...] = _layer_norm(alpha * h + ff, ln_g_ref[...], ln_b_ref[...])


def _mlp_call(h2, params, layer, alpha):
    n, d = h2.shape
    rows = MLP_ROW_BLOCK
    assert n % rows == 0
    h_spec = pl.BlockSpec((rows, d), lambda i: (i, 0))
    return pl.pallas_call(
        functools.partial(_mlp_kernel, alpha),
        grid=(n // rows,),
        in_specs=[h_spec] + [_layer_spec(p, layer, 1) for p in params],
        out_specs=h_spec,
        out_shape=jax.ShapeDtypeStruct(h2.shape, F32),
        compiler_params=pltpu.CompilerParams(
            dimension_semantics=("arbitrary",), vmem_limit_bytes=VMEM_LIMIT),
        name="mlp",
    )(h2, *params)


def _block_diag(blocks):
    g, r, c = blocks.shape
    tiled = jnp.tile(blocks.reshape(g * r, c), (1, g))
    rg = lax.broadcasted_iota(jnp.int32, (g * r, g * c), 0) // r
    cg = lax.broadcasted_iota(jnp.int32, (g * r, g * c), 1) // c
    return jnp.where(rg == cg, tiled, 0.0)


def _s5_params(a_re, a_im, log_dt, b_re, b_im, c_re, c_im):
    lam_re = jnp.minimum(a_re, S5_MAX_RE)
    lam_im = a_im
    dt = jnp.exp(log_dt)[:, None]
    mag = jnp.exp(lam_re * dt)
    lb_re = mag * jnp.cos(lam_im * dt)
    lb_im = mag * jnp.sin(lam_im * dt)
    den = jnp.square(lam_re) + jnp.square(lam_im)
    nr = lb_re - 1.0
    s_re = (nr * lam_re + lb_im * lam_im) / den
    s_im = (lb_im * lam_re - nr * lam_im) / den
    bb_re = s_re[..., None] * b_re - s_im[..., None] * b_im
    bb_im = s_re[..., None] * b_im + s_im[..., None] * b_re
    steps = jnp.arange(1, SUBLANE + 1, dtype=F32)[:, None, None]
    pmag = jnp.exp(lam_re * dt * steps)
    pw_re = pmag * jnp.cos(lam_im * dt * steps)
    pw_im = pmag * jnp.sin(lam_im * dt * steps)
    b_rows = []
    for d in range(4):
        if d == 0:
            d_re, d_im = bb_re, bb_im
        else:
            pr, pi = pw_re[d - 1][..., None], pw_im[d - 1][..., None]
            d_re, d_im = bb_re * pr - bb_im * pi, bb_re * pi + bb_im * pr
        b_rows.append(jnp.concatenate([_block_diag(d_re.transpose(0, 2, 1)), _block_diag(d_im.transpose(0, 2, 1))], axis=1))
    b_bd = jnp.concatenate(b_rows, axis=0)
    c_bd = jnp.concatenate([_block_diag(c_re.transpose(0, 2, 1)), -_block_diag(c_im.transpose(0, 2, 1))], axis=0)
    return b_bd.astype(BF16), c_bd.astype(BF16), pw_re.reshape(SUBLANE, C_S), pw_im.reshape(SUBLANE, C_S)


def kernel(x, meta_tokens, w_in, hgrn_lb_logits, hgrn_norm_w, m2_conv_w, m2_conv_b, m2_dt_bias, m2_a_log, m2_d, m2_norm_w, s5_a_re, s5_a_im, s5_log_dt, s5_b_re, s5_b_im, s5_c_re, s5_c_im, s5_d, s5_glu_w, s5_glu_b, w_out, ln1_g, ln1_b, w_mlp_in, w_mlp_out, ln2_g, ln2_b):
    bsz, seq, d = x.shape
    depth = w_in.shape[0]
    alpha = (2 * depth) ** 0.25
    lp = -(-(N_META + seq) // ROW_BLOCK) * ROW_BLOCK
    front_pad = lp - N_META - seq

    meta = jnp.broadcast_to(meta_tokens.astype(x.dtype)[None], (bsz, N_META, d))
    hp = jnp.concatenate([jnp.zeros((bsz, front_pad, d), x.dtype), meta, x], axis=1)

    f32 = lambda v: v.astype(F32)
    row3 = lambda v: f32(v).reshape(depth, 1, -1)
    lb_cum = jnp.cumsum(jax.nn.softmax(f32(hgrn_lb_logits), axis=0), axis=0)
    lower = lb_cum - lb_cum[0]
    pad_cols = D_IN_PACKED - O_DT - B_HEADS
    w_in_p = jnp.concatenate([w_in[:, :, :2816], w_in[:, :, 2822:3078], w_in[:, :, 2816:2822],
                              jnp.zeros((depth, d, pad_cols), w_in.dtype)], axis=2).astype(BF16)
    b_bd, c_bd, pow_re, pow_im = jax.vmap(_s5_params)(
        f32(s5_a_re), f32(s5_a_im), f32(s5_log_dt), f32(s5_b_re), f32(s5_b_im), f32(s5_c_re), f32(s5_c_im))
    pad6 = lambda v: jnp.concatenate([f32(v), jnp.zeros((depth, LANE - B_HEADS), F32)], axis=1).reshape(depth, 1, LANE)
    mixer_params = [
        w_in_p, row3(jnp.log(lower)), row3(1.0 - lower), row3(jnp.tile(hgrn_norm_w, (1, A_HEADS))),
        f32(m2_conv_w), row3(m2_conv_b), pad6(m2_dt_bias), pad6(m2_a_log),
        row3(jnp.repeat(m2_d, B_P, axis=1)), row3(m2_norm_w),
        b_bd, c_bd, pow_re, pow_im, row3(s5_d), s5_glu_w.astype(BF16), row3(s5_glu_b),
        w_out.astype(BF16), row3(ln1_g), row3(ln1_b),
    ]
    mlp_params = [w_mlp_in.astype(BF16), w_mlp_out.astype(BF16), row3(ln2_g), row3(ln2_b)]

    for l in range(depth):
        hp = _mixer_call(hp, mixer_params, l, alpha, front_pad)
        hp = _mlp_call(hp.reshape(bsz * lp, d), mlp_params, l, alpha).reshape(bsz, lp, d)
    return hp[:, front_pad + N_META:]
```

```python
import functools

import jax
import jax.numpy as jnp
from jax import lax
from jax.experimental import pallas as pl
from jax.experimental.pallas import tpu as pltpu

F32 = jnp.float32
BF16 = jnp.bfloat16

D_MODEL = 1024
N_META = 16
A_HEADS, A_DK, A_W = 6, 64, 384
B_HEADS, B_P, B_W, B_G, B_N = 6, 64, 384, 2, 128
B_CONV = 4
C_G, C_CH, C_W, C_N = 16, 16, 256, 64
C_S = C_G * C_N
D_FF = 4 * D_MODEL
LN_EPS = 1e-5
RMS_EPS = 1e-6
S5_MAX_RE = -1e-4
LOG2_E = 1.4426950408889634

O_Q, O_F, O_I, O_G, O_Z, O_XBC, O_U, O_DT = 0, 384, 768, 1152, 1536, 1920, 2816, 3072
D_IN_PACKED = 3200
XBC_W = 896

ROW_BLOCK = 256
MLP_CHUNK = 512
LANE = 128
SUBLANE = 8
VMEM_LIMIT = 56 * 1024 * 1024


def _sigmoid(x):
    return 0.5 * jnp.tanh(0.5 * x) + 0.5


def _silu(x):
    hx = 0.5 * x
    return hx + hx * jnp.tanh(hx)


def _log1p_exp_neg_abs(x):
    return jnp.log(1.0 + jnp.exp(-jnp.abs(x)))


def _softplus(x):
    return jnp.maximum(x, 0.0) + _log1p_exp_neg_abs(x)


def _split_bf16(x, parts):
    out = []
    r = x
    for i in range(parts):
        p = r.astype(BF16)
        out.append(p)
        if i + 1 < parts:
            r = r - p.astype(F32)
    return out


def _dot(a, b):
    return jnp.dot(a, b, preferred_element_type=F32)


def _dot_nt(a, b):
    return lax.dot_general(a, b, (((1,), (1,)), ((), ())), preferred_element_type=F32)


def _dot_tn(a, b):
    return lax.dot_general(a, b, (((0,), (0,)), ((), ())), preferred_element_type=F32)


def _shift_rows(x, n):
    rows = x.shape[0]
    if n == 0:
        return x
    if n % SUBLANE == 0:
        z = jnp.zeros((abs(n), x.shape[1]), x.dtype)
        if n > 0:
            return jnp.concatenate([z, x[: rows - n]], axis=0)
        return jnp.concatenate([x[-n:], z], axis=0)
    return pltpu.roll(x, n % rows, axis=0)


def _layer_norm(x, g, b):
    mu = jnp.mean(x, axis=-1, keepdims=True)
    xc = x - mu
    var = jnp.mean(xc * xc, axis=-1, keepdims=True)
    return xc * lax.rsqrt(var + LN_EPS) * g + b


def _hgrn2_pair(proj, pair, log_lb, one_m_lb, norm_w, pad_row, group_masks, st_ref):
    sl = slice(LANE * pair, LANE * (pair + 1))
    col = lambda off: proj(off + LANE * pair, off + LANE * (pair + 1))
    f_raw = col(O_F)
    q_raw = col(O_Q)
    rows = f_raw.shape[0]
    assert rows == 256
    yield
    q = _silu(q_raw)
    lse = _log1p_exp_neg_abs(f_raw)
    ls_pos = -(jnp.maximum(-f_raw, 0.0) + lse)
    ls_neg = -(jnp.maximum(f_raw, 0.0) + lse)
    b_term = log_lb[:, sl] + ls_neg
    lf = jnp.maximum(ls_pos, b_term) + _log1p_exp_neg_abs(ls_pos - b_term)
    lf = jnp.where(pad_row, 0.0, lf)
    k = jnp.where(pad_row, 0.0, one_m_lb[:, sl] * jnp.exp(ls_neg))
    yield

    row = lax.broadcasted_iota(jnp.int32, (rows, 1), 0)
    lane = lax.broadcasted_iota(jnp.int32, (1, LANE), 1)
    lo = lane < 64
    zero_h = jnp.zeros((rows, LANE), BF16)

    def pack_heads(s1, s2, s3):
        s1r = pltpu.roll(s1, 64, axis=1)
        s2r = pltpu.roll(s2, 64, axis=1)
        even = jnp.concatenate([jnp.where(lo, s1, s2r), jnp.where(lo, s3, zero_h)], axis=1)
        odd = jnp.concatenate([jnp.where(lo, s1r, s2), jnp.where(lo, zero_h, s3)], axis=1)
        return even, odd

    lf = lf * LOG2_E
    cs, sf, tot = lf, jnp.zeros_like(lf), lf
    scores = {}
    for c in (1, 4, 16, 64):
        j = (row // c) % 4
        qt = (q * jnp.exp2(cs)).astype(BF16)
        up1, up2 = _shift_rows(tot, -c), _shift_rows(tot, -2 * c)
        e2 = sf + up1
        e3 = e2 + up2
        ke1, ke2, ke3 = [(k * jnp.exp2(e)).astype(BF16) for e in (sf, e2, e3)]
        qa, qb = pack_heads(*[jnp.where(j == i, qt, zero_h) for i in (1, 2, 3)])
        ka, kb = pack_heads(jnp.where(j == 0, ke1, zero_h),
                            jnp.where(j == 1, ke1, jnp.where(j == 0, ke2, zero_h)),
                            jnp.where(j == 2, ke1, jnp.where(j == 1, ke2, jnp.where(j == 0, ke3, zero_h))))
        scores[c] = (_dot_nt(qa, ka), _dot_nt(qb, kb))
        yield
        new_cs = cs + (jnp.where(j >= 1, _shift_rows(tot, c), 0.0) + jnp.where(j >= 2, _shift_rows(tot, 2 * c), 0.0)
                       + jnp.where(j >= 3, _shift_rows(tot, 3 * c), 0.0))
        new_sf = sf + (jnp.where(j <= 2, up1, 0.0) + jnp.where(j <= 1, up2, 0.0)
                       + jnp.where(j <= 0, _shift_rows(tot, -3 * c), 0.0))
        cs, sf = new_cs, new_sf
        tot = cs + sf
        yield

    q_in = (q * jnp.exp2(cs)).astype(BF16)
    k_out = (k * jnp.exp2(sf)).astype(BF16)
    decay_blk = jnp.exp2(tot[0:1, :])
    qp, kp = q.astype(BF16), k.astype(BF16)
    zero_b = jnp.zeros_like(qp)
    s_d = (_dot_nt(jnp.where(lo, qp, zero_b), kp), _dot_nt(jnp.where(lo, zero_b, qp), kp))
    v_raw = col(O_I)
    vp = v_raw.astype(BF16)
    yield

    same64, same16, same4, diag = group_masks
    sc = []
    for hh in range(2):
        sc.append(jnp.where(same4, jnp.where(diag, s_d[hh], scores[1][hh]),
                            jnp.where(same16, scores[4][hh],
                                      jnp.where(same64, scores[16][hh], scores[64][hh]))).astype(BF16))
        yield
    st = st_ref[pair]
    vv = jnp.concatenate([jnp.where(lo, vp, zero_b), jnp.where(lo, zero_b, vp)], axis=0)
    o = _dot(jnp.concatenate(sc, axis=1), vv) + _dot_nt(q_in, st.astype(BF16))
    upd = _dot_tn(vp, k_out)
    r2 = lax.broadcasted_iota(jnp.int32, (LANE, LANE), 0)
    c2 = lax.broadcasted_iota(jnp.int32, (LANE, LANE), 1)
    head_diag = (r2 // 64) == (c2 // 64)
    st_ref[pair] = st * decay_blk + jnp.where(head_diag, upd, 0.0)
    g_raw = col(O_G)
    yield

    head_ones = jnp.where(head_diag, 1.0, 0.0).astype(BF16)
    ms = _dot((o * o).astype(BF16), head_ones) * (1.0 / A_DK)
    return o * lax.rsqrt(ms + RMS_EPS) * norm_w[:, sl] * _silu(g_raw)


def _ssd(proj, conv_w, conv_b, dt_bias, a_log, d_full, norm_w, pad_row, hist_ref, st_ref):
    xbc_raw = proj(O_XBC, O_U)
    dt_raw = proj(O_DT, D_IN_PACKED)
    rows = xbc_raw.shape[0]
    yield
    cat = jnp.concatenate([hist_ref[...], xbc_raw], axis=0)
    hist_ref[...] = xbc_raw[rows - SUBLANE:, :]
    acc = conv_b + conv_w[B_CONV - 1:B_CONV, :] * xbc_raw
    for n in (1, 2, 3):
        shifted = pltpu.roll(cat, n, axis=0)[SUBLANE:, :]
        acc = acc + conv_w[B_CONV - 1 - n:B_CONV - n, :] * shifted
    yield
    xbc = _silu(acc)
    xs = xbc[:, :B_W]
    bm = xbc[:, B_W:B_W + B_G * B_N].astype(BF16)
    cm = xbc[:, B_W + B_G * B_N:].astype(BF16)

    lane = lax.broadcasted_iota(jnp.int32, (1, LANE), 1)
    dt = jnp.where(pad_row, 0.0, _softplus(dt_raw + dt_bias))
    a_neg = jnp.where(lane < B_HEADS, -jnp.exp(a_log), 0.0)
    d_a = dt * (a_neg * LOG2_E)
    yield

    rr = lax.broadcasted_iota(jnp.int32, (rows, rows), 0)
    cc = lax.broadcasted_iota(jnp.int32, (rows, rows), 1)
    causal = cc <= rr
    tril = jnp.where(causal, 1.0, 0.0).astype(BF16)
    triu = jnp.where(rr <= cc, 1.0, 0.0).astype(BF16)
    d_a_parts = jnp.concatenate(_split_bf16(d_a, 3), axis=0)
    cum = _dot(jnp.concatenate([tril] * 3, axis=1), d_a_parts)
    cum_t = _dot_tn(d_a_parts, jnp.concatenate([triu] * 3, axis=0))
    yield

    er = lax.broadcasted_iota(jnp.int32, (3 * LANE, B_W), 0)
    ec = lax.broadcasted_iota(jnp.int32, (3 * LANE, B_W), 1)
    expand = jnp.where(er % LANE == ec // B_P, 1.0, 0.0).astype(BF16)
    dt_full = _dot(jnp.concatenate(_split_bf16(dt, 3), axis=1), expand)
    cum_full = _dot(jnp.concatenate(_split_bf16(cum, 3), axis=1), expand)
    xdt = xs * dt_full
    lo = lane < 64
    cb = [_dot_nt(cm[:, B_N * g:B_N * (g + 1)], bm[:, B_N * g:B_N * (g + 1)]) for g in range(B_G)]
    yield

    y_parts = []
    for p in range(B_HEADS // 2):
        sc = []
        for h in (2 * p, 2 * p + 1):
            diff = cum[:, h:h + 1] - cum_t[h:h + 1, :]
            decay = jnp.exp2(jnp.where(causal, diff, -jnp.inf))
            sc.append((cb[h // (B_HEADS // B_G)] * decay).astype(BF16))
        xp = xdt[:, LANE * p:LANE * (p + 1)]
        zero = jnp.zeros_like(xp)
        xx = jnp.concatenate([jnp.where(lo, xp, zero), jnp.where(lo, zero, xp)], axis=0).astype(BF16)
        y_parts.append(_dot(jnp.concatenate(sc, axis=1), xx))
        yield
    y = jnp.concatenate(y_parts, axis=1)

    st = st_ref[...]
    y = y + _dot(cm, st.astype(BF16)) * jnp.exp2(cum_full) + d_full * xs
    cum_last = cum_full[rows - 1:rows, :]
    upd = _dot_tn(bm, (xdt * jnp.exp2(cum_last - cum_full)).astype(BF16))
    sr = lax.broadcasted_iota(jnp.int32, (B_G * B_N, B_W), 0)
    sc = lax.broadcasted_iota(jnp.int32, (B_G * B_N, B_W), 1)
    st_ref[...] = st * jnp.exp2(cum_last) + jnp.where((sr // B_N) == (sc // (B_W // B_G)), upd, 0.0)
    z_raw = proj(O_Z, O_XBC)
    yield

    y = y * _silu(z_raw)
    gr = lax.broadcasted_iota(jnp.int32, (B_W, B_W), 0)
    gc = lax.broadcasted_iota(jnp.int32, (B_W, B_W), 1)
    group_ones = jnp.where((gr // (B_W // B_G)) == (gc // (B_W // B_G)), 1.0, 0.0).astype(BF16)
    ms = _dot((y * y).astype(BF16), group_ones) * (1.0 / (B_W // B_G))
    return y * lax.rsqrt(ms + RMS_EPS) * norm_w


def _s5(proj, b_bd, c_bd, pow_re, pow_im, d_skip, glu_w, glu_b, carry_ref):
    u = proj(O_U, O_DT)
    rows = u.shape[0]
    groups = rows // SUBLANE
    sub = lax.broadcasted_iota(jnp.int32, (SUBLANE, 1), 0)
    u3 = u.reshape(groups, SUBLANE, C_W)
    delayed = [u] + [jnp.where(sub >= d, pltpu.roll(u3, d, axis=1), 0.0).reshape(rows, C_W) for d in (1, 2, 3)]
    bu = _dot(jnp.concatenate([x.astype(BF16) for x in delayed], axis=1), b_bd)
    yield
    x_re = bu[:, :C_S].reshape(groups, SUBLANE, C_S)
    x_im = bu[:, C_S:].reshape(groups, SUBLANE, C_S)
    for d in (4,):
        keep = sub >= d
        p_re = jnp.where(keep, pow_re[d - 1:d, :], 0.0)[None]
        p_im = jnp.where(keep, pow_im[d - 1:d, :], 0.0)[None]
        s_re = pltpu.roll(x_re, d, axis=1)
        s_im = pltpu.roll(x_im, d, axis=1)
        x_re, x_im = (x_re + (p_re * s_re - p_im * s_im), x_im + (p_re * s_im + p_im * s_re))
        yield

    c_re, c_im = carry_ref[0], carry_ref[1]
    g_re, g_im = [], []
    for i in range(groups):
        g_re.append(x_re[i] + (pow_re * c_re - pow_im * c_im))
        g_im.append(x_im[i] + (pow_re * c_im + pow_im * c_re))
        c_re, c_im = g_re[-1][SUBLANE - 1:, :], g_im[-1][SUBLANE - 1:, :]
        if i % 4 == 3:
            yield
    carry_ref[0] = c_re
    carry_ref[1] = c_im

    y = (_dot(jnp.concatenate(g_re, axis=0).astype(BF16), c_bd[:C_S, :])
         + _dot(jnp.concatenate(g_im, axis=0).astype(BF16), c_bd[C_S:, :]) + d_skip * u)
    yield
    y = jax.nn.gelu(y, approximate=True)
    gate = _dot(y.astype(BF16), glu_w) + glu_b
    yield
    return y * _sigmoid(gate)


def _interleave(gens):
    results = [None] * len(gens)
    live = list(range(len(gens)))
    while live:
        for i in list(live):
            try:
                next(gens[i])
            except StopIteration as stop:
                results[i] = stop.value
                live.remove(i)
    return results


def _mlp(prev, w1_ref, w2_ref, ln_g, ln_b, alpha):
    hb = prev.astype(BF16)
    ff = None
    for j in range(D_FF // MLP_CHUNK):
        cols = slice(j * MLP_CHUNK, (j + 1) * MLP_CHUNK)
        hid = jnp.maximum(_dot(hb, w1_ref[:, cols]), 0.0)
        yield
        part = _dot((hid * hid).astype(BF16), w2_ref[cols, :])
        ff = part if ff is None else ff + part
        yield
    return _layer_norm(alpha * prev + ff, ln_g, ln_b)


def _layer_kernel(alpha, front_pad, blocks_per_seq, h_ref, w_in_ref, log_lb_ref, one_m_lb_ref, a_norm_ref,
                  conv_w_ref, conv_b_ref, dt_bias_ref, a_log_ref, d_full_ref, b_norm_ref,
                  s5_b_ref, s5_c_ref, s5_pre_ref, s5_pim_ref, s5_d_ref, glu_w_ref, glu_b_ref,
                  w_out_ref, ln1_g_ref, ln1_b_ref, w1_ref, w2_ref, ln2_g_ref, ln2_b_ref, o_ref,
                  a_state, b_hist, b_state, c_carry, h1_prev):
    step = pl.program_id(0)
    blk = step % blocks_per_seq
    rows = h_ref.shape[0]

    @pl.when(step == 0)
    def _():
        h1_prev[...] = jnp.zeros_like(h1_prev)

    @pl.when(blk == 0)
    def _():
        a_state[...] = jnp.zeros_like(a_state)
        b_hist[...] = jnp.zeros_like(b_hist)
        b_state[...] = jnp.zeros_like(b_state)
        c_carry[...] = jnp.zeros_like(c_carry)

    row = lax.broadcasted_iota(jnp.int32, (rows, 1), 0) + blk * rows
    pad_row = row < front_pad
    h = h_ref[...]
    hb = jnp.where(pad_row, 0.0, h).astype(BF16)

    groups = [(O_Q, O_I), (O_U, D_IN_PACKED), (O_Z, O_U), (O_I, O_Z)]
    projected = [(lo_col, hi_col, _dot(hb, w_in_ref[:, lo_col:hi_col])) for lo_col, hi_col in groups]

    def proj(lo_col, hi_col):
        for g_lo, g_hi, arr in projected:
            if g_lo <= lo_col and hi_col <= g_hi:
                return arr[:, lo_col - g_lo:hi_col - g_lo]
        raise ValueError((lo_col, hi_col))

    gens = [_s5(proj, s5_b_ref[...], s5_c_ref[...], s5_pre_ref[...], s5_pim_ref[...],
                s5_d_ref[...], glu_w_ref[...], glu_b_ref[...], c_carry),
            _ssd(proj, conv_w_ref[...], conv_b_ref[...], dt_bias_ref[...], a_log_ref[...],
                 d_full_ref[...], b_norm_ref[...], pad_row, b_hist, b_state)]
    tx = lax.broadcasted_iota(jnp.int32, (rows, rows), 0) ^ lax.broadcasted_iota(jnp.int32, (rows, rows), 1)
    group_masks = (tx < 64, tx < 16, tx < 4, tx == 0)
    gens += [_hgrn2_pair(proj, p, log_lb_ref[...], one_m_lb_ref[...], a_norm_ref[...], pad_row, group_masks, a_state)
             for p in range(A_HEADS // 2)]
    gens.append(_mlp(h1_prev[...], w1_ref, w2_ref, ln2_g_ref[...], ln2_b_ref[...], alpha))
    y_c, y_b, *y_a, out_prev = _interleave(gens)
    o_ref[...] = out_prev
    y_all = jnp.concatenate([y.astype(BF16) for y in (*y_a, y_b, y_c)], axis=1)
    h1_prev[...] = _layer_norm(alpha * h + _dot(y_all, w_out_ref[...]), ln1_g_ref[...], ln1_b_ref[...])


def _layer_spec(arr, layer):
    return pl.BlockSpec((None,) + arr.shape[1:], lambda i: (layer, 0, 0), pipeline_mode=pl.Buffered(1))


def _layer_call(h2, params, layer, alpha, front_pad, blocks_per_seq):
    n, d = h2.shape
    rows = ROW_BLOCK
    n_blocks = n // rows
    assert n % rows == 0 and n_blocks % blocks_per_seq == 0
    return pl.pallas_call(
        functools.partial(_layer_kernel, alpha, front_pad, blocks_per_seq),
        grid=(n_blocks + 1,),
        in_specs=[pl.BlockSpec((rows, d), lambda i: (jnp.minimum(i, n_blocks - 1), 0))]
        + [_layer_spec(p, layer) for p in params],
        out_specs=pl.BlockSpec((rows, d), lambda i: (jnp.maximum(i - 1, 0), 0)),
        out_shape=jax.ShapeDtypeStruct(h2.shape, F32),
        scratch_shapes=[
            pltpu.VMEM((A_HEADS // 2, LANE, LANE), F32),
            pltpu.VMEM((SUBLANE, XBC_W), F32),
            pltpu.VMEM((B_G * B_N, B_W), F32),
            pltpu.VMEM((2, 1, C_S), F32),
            pltpu.VMEM((rows, d), F32),
        ],
        compiler_params=pltpu.CompilerParams(
            dimension_semantics=("arbitrary",), vmem_limit_bytes=VMEM_LIMIT),
        name="layer",
    )(h2, *params)


def _block_diag(blocks):
    g, r, c = blocks.shape
    tiled = jnp.tile(blocks.reshape(g * r, c), (1, g))
    rg = lax.broadcasted_iota(jnp.int32, (g * r, g * c), 0) // r
    cg = lax.broadcasted_iota(jnp.int32, (g * r, g * c), 1) // c
    return jnp.where(rg == cg, tiled, 0.0)


def _s5_params(a_re, a_im, log_dt, b_re, b_im, c_re, c_im):
    lam_re = jnp.minimum(a_re, S5_MAX_RE)
    lam_im = a_im
    dt = jnp.exp(log_dt)[:, None]
    mag = jnp.exp(lam_re * dt)
    lb_re = mag * jnp.cos(lam_im * dt)
    lb_im = mag * jnp.sin(lam_im * dt)
    den = jnp.square(lam_re) + jnp.square(lam_im)
    nr = lb_re - 1.0
    s_re = (nr * lam_re + lb_im * lam_im) / den
    s_im = (lb_im * lam_re - nr * lam_im) / den
    bb_re = s_re[..., None] * b_re - s_im[..., None] * b_im
    bb_im = s_re[..., None] * b_im + s_im[..., None] * b_re
    steps = jnp.arange(1, SUBLANE + 1, dtype=F32)[:, None, None]
    pmag = jnp.exp(lam_re * dt * steps)
    pw_re = pmag * jnp.cos(lam_im * dt * steps)
    pw_im = pmag * jnp.sin(lam_im * dt * steps)
    b_rows = []
    for d in range(4):
        if d == 0:
            d_re, d_im = bb_re, bb_im
        else:
            pr, pi = pw_re[d - 1][..., None], pw_im[d - 1][..., None]
            d_re, d_im = bb_re * pr - bb_im * pi, bb_re * pi + bb_im * pr
        b_rows.append(jnp.concatenate([_block_diag(d_re.transpose(0, 2, 1)), _block_diag(d_im.transpose(0, 2, 1))], axis=1))
    b_bd = jnp.concatenate(b_rows, axis=0)
    c_bd = jnp.concatenate([_block_diag(c_re.transpose(0, 2, 1)), -_block_diag(c_im.transpose(0, 2, 1))], axis=0)
    return b_bd.astype(BF16), c_bd.astype(BF16), pw_re.reshape(SUBLANE, C_S), pw_im.reshape(SUBLANE, C_S)


def kernel(x, meta_tokens, w_in, hgrn_lb_logits, hgrn_norm_w, m2_conv_w, m2_conv_b, m2_dt_bias, m2_a_log, m2_d, m2_norm_w, s5_a_re, s5_a_im, s5_log_dt, s5_b_re, s5_b_im, s5_c_re, s5_c_im, s5_d, s5_glu_w, s5_glu_b, w_out, ln1_g, ln1_b, w_mlp_in, w_mlp_out, ln2_g, ln2_b):
    bsz, seq, d = x.shape
    depth = w_in.shape[0]
    alpha = (2 * depth) ** 0.25
    lp = -(-(N_META + seq) // ROW_BLOCK) * ROW_BLOCK
    front_pad = lp - N_META - seq

    meta = jnp.broadcast_to(meta_tokens.astype(x.dtype)[None], (bsz, N_META, d))
    hp = jnp.concatenate([jnp.zeros((bsz, front_pad, d), x.dtype), meta, x], axis=1)

    f32 = lambda v: v.astype(F32)
    row3 = lambda v: f32(v).reshape(depth, 1, -1)
    lb_cum = jnp.cumsum(jax.nn.softmax(f32(hgrn_lb_logits), axis=0), axis=0)
    lower = lb_cum - lb_cum[0]
    pad_cols = D_IN_PACKED - O_DT - B_HEADS
    w_in_p = jnp.concatenate([w_in[:, :, :2816], w_in[:, :, 2822:3078], w_in[:, :, 2816:2822],
                              jnp.zeros((depth, d, pad_cols), w_in.dtype)], axis=2).astype(BF16)
    b_bd, c_bd, pow_re, pow_im = jax.vmap(_s5_params)(
        f32(s5_a_re), f32(s5_a_im), f32(s5_log_dt), f32(s5_b_re), f32(s5_b_im), f32(s5_c_re), f32(s5_c_im))
    pad6 = lambda v: jnp.concatenate([f32(v), jnp.zeros((depth, LANE - B_HEADS), F32)], axis=1).reshape(depth, 1, LANE)
    params = [
        w_in_p, row3(jnp.log(lower)), row3(1.0 - lower), row3(jnp.tile(hgrn_norm_w, (1, A_HEADS))),
        f32(m2_conv_w), row3(m2_conv_b), pad6(m2_dt_bias), pad6(m2_a_log),
        row3(jnp.repeat(m2_d, B_P, axis=1)), row3(m2_norm_w),
        b_bd, c_bd, pow_re, pow_im, row3(s5_d), s5_glu_w.astype(BF16), row3(s5_glu_b),
        w_out.astype(BF16), row3(ln1_g), row3(ln1_b),
        w_mlp_in.astype(BF16), w_mlp_out.astype(BF16), row3(ln2_g), row3(ln2_b),
    ]

    h2 = hp.reshape(bsz * lp, d)
    for l in range(depth):
        h2 = _layer_call(h2, params, l, alpha, front_pad, lp // ROW_BLOCK)
    return h2.reshape(bsz, lp, d)[:, front_pad + N_META:]
```

```python
import functools

import jax
import jax.numpy as jnp
from jax import lax
from jax.experimental import pallas as pl
from jax.experimental.pallas import tpu as pltpu

F32 = jnp.float32
BF16 = jnp.bfloat16

D_MODEL = 1024
N_META = 16
A_HEADS, A_DK, A_W = 6, 64, 384
B_HEADS, B_P, B_W, B_G, B_N = 6, 64, 384, 2, 128
B_CONV = 4
C_G, C_CH, C_W, C_N = 16, 16, 256, 64
C_S = C_G * C_N
D_FF = 4 * D_MODEL
LN_EPS = 1e-5
RMS_EPS = 1e-6
S5_MAX_RE = -1e-4
LOG2_E = 1.4426950408889634

O_Q, O_F, O_I, O_G, O_Z, O_XBC, O_U, O_DT = 0, 384, 768, 1152, 1536, 1920, 2816, 3072
D_IN_PACKED = 3200
XBC_W = 896

ROW_BLOCK = 256
MLP_CHUNK = 512
LANE = 128
SUBLANE = 8
VMEM_LIMIT = 56 * 1024 * 1024


def _sigmoid(x):
    return 0.5 * jnp.tanh(0.5 * x) + 0.5


def _silu(x):
    hx = 0.5 * x
    return hx + hx * jnp.tanh(hx)


def _log1p_exp_neg_abs(x):
    return jnp.log(1.0 + jnp.exp(-jnp.abs(x)))


def _softplus(x):
    return jnp.maximum(x, 0.0) + _log1p_exp_neg_abs(x)


def _split_bf16(x, parts):
    out = []
    r = x
    for i in range(parts):
        p = r.astype(BF16)
        out.append(p)
        if i + 1 < parts:
            r = r - p.astype(F32)
    return out


def _dot(a, b):
    return jnp.dot(a, b, preferred_element_type=F32)


def _dot_nt(a, b):
    return lax.dot_general(a, b, (((1,), (1,)), ((), ())), preferred_element_type=F32)


def _dot_tn(a, b):
    return lax.dot_general(a, b, (((0,), (0,)), ((), ())), preferred_element_type=F32)


def _shift_rows(x, n):
    rows = x.shape[0]
    if n == 0:
        return x
    if n % SUBLANE == 0:
        z = jnp.zeros((abs(n), x.shape[1]), x.dtype)
        if n > 0:
            return jnp.concatenate([z, x[: rows - n]], axis=0)
        return jnp.concatenate([x[-n:], z], axis=0)
    return pltpu.roll(x, n % rows, axis=0)


def _layer_norm(x, g, b):
    mu = jnp.mean(x, axis=-1, keepdims=True)
    xc = x - mu
    var = jnp.mean(xc * xc, axis=-1, keepdims=True)
    return xc * lax.rsqrt(var + LN_EPS) * g + b


def _hgrn2_pair(proj, pair, log_lb, one_m_lb, norm_w, pad_row, group_masks, st_ref):
    sl = slice(LANE * pair, LANE * (pair + 1))
    col = lambda off: proj(off + LANE * pair, off + LANE * (pair + 1))
    f_raw = col(O_F)
    q_raw = col(O_Q)
    rows = f_raw.shape[0]
    assert rows == 256
    yield
    q = _silu(q_raw)
    lse = _log1p_exp_neg_abs(f_raw)
    ls_pos = -(jnp.maximum(-f_raw, 0.0) + lse)
    ls_neg = -(jnp.maximum(f_raw, 0.0) + lse)
    b_term = log_lb[:, sl] + ls_neg
    lf = jnp.maximum(ls_pos, b_term) + _log1p_exp_neg_abs(ls_pos - b_term)
    lf = jnp.where(pad_row, 0.0, lf)
    k = jnp.where(pad_row, 0.0, one_m_lb[:, sl] * jnp.exp(ls_neg))
    yield

    row = lax.broadcasted_iota(jnp.int32, (rows, 1), 0)
    lane = lax.broadcasted_iota(jnp.int32, (1, LANE), 1)
    lo = lane < 64
    zero_h = jnp.zeros((rows, LANE), BF16)

    def pack_heads(s1, s2, s3):
        s1r = pltpu.roll(s1, 64, axis=1)
        s2r = pltpu.roll(s2, 64, axis=1)
        even = jnp.concatenate([jnp.where(lo, s1, s2r), jnp.where(lo, s3, zero_h)], axis=1)
        odd = jnp.concatenate([jnp.where(lo, s1r, s2), jnp.where(lo, zero_h, s3)], axis=1)
        return even, odd

    lf = lf * LOG2_E
    cs, sf, tot = lf, jnp.zeros_like(lf), lf
    scores = {}
    for c in (1, 4, 16, 64):
        j = (row // c) % 4
        qt = (q * jnp.exp2(cs)).astype(BF16)
        up1, up2 = _shift_rows(tot, -c), _shift_rows(tot, -2 * c)
        e2 = sf + up1
        e3 = e2 + up2
        ke1, ke2, ke3 = [(k * jnp.exp2(e)).astype(BF16) for e in (sf, e2, e3)]
        qa, qb = pack_heads(*[jnp.where(j == i, qt, zero_h) for i in (1, 2, 3)])
        ka, kb = pack_heads(jnp.where(j == 0, ke1, zero_h),
                            jnp.where(j == 1, ke1, jnp.where(j == 0, ke2, zero_h)),
                            jnp.where(j == 2, ke1, jnp.where(j == 1, ke2, jnp.where(j == 0, ke3, zero_h))))
        scores[c] = (_dot_nt(qa, ka), _dot_nt(qb, kb))
        yield
        new_cs = cs + (jnp.where(j >= 1, _shift_rows(tot, c), 0.0) + jnp.where(j >= 2, _shift_rows(tot, 2 * c), 0.0)
                       + jnp.where(j >= 3, _shift_rows(tot, 3 * c), 0.0))
        new_sf = sf + (jnp.where(j <= 2, up1, 0.0) + jnp.where(j <= 1, up2, 0.0)
                       + jnp.where(j <= 0, _shift_rows(tot, -3 * c), 0.0))
        cs, sf = new_cs, new_sf
        tot = cs + sf
        yield

    q_in = (q * jnp.exp2(cs)).astype(BF16)
    k_out = (k * jnp.exp2(sf)).astype(BF16)
    decay_blk = jnp.exp2(tot[0:1, :])
    v_raw = col(O_I)
    vp = v_raw.astype(BF16)
    zero_b = jnp.zeros_like(vp)
    r2 = lax.broadcasted_iota(jnp.int32, (LANE, LANE), 0)
    c2 = lax.broadcasted_iota(jnp.int32, (LANE, LANE), 1)
    head_diag = (r2 // 64) == (c2 // 64)
    head_ones = jnp.where(head_diag, 1.0, 0.0).astype(BF16)
    o_diag = _dot((q * k).astype(BF16), head_ones) * vp.astype(F32)
    yield

    same64, same16, same4 = group_masks
    sc = []
    for hh in range(2):
        sc.append(jnp.where(same4, scores[1][hh],
                            jnp.where(same16, scores[4][hh],
                                      jnp.where(same64, scores[16][hh], scores[64][hh]))).astype(BF16))
        yield
    st = st_ref[pair]
    vv = jnp.concatenate([jnp.where(lo, vp, zero_b), jnp.where(lo, zero_b, vp)], axis=0)
    o = _dot(jnp.concatenate(sc, axis=1), vv) + _dot_nt(q_in, st.astype(BF16)) + o_diag
    upd = _dot_tn(vp, k_out)
    st_ref[pair] = st * decay_blk + jnp.where(head_diag, upd, 0.0)
    g_raw = col(O_G)
    yield

    ms =_dot((o * o).astype(BF16), head_ones) * (1.0 / A_DK)
    return o * lax.rsqrt(ms + RMS_EPS) * norm_w[:, sl] * _silu(g_raw)


def _ssd(proj, conv_w, conv_b, dt_bias, a_log, d_full, norm_w, pad_row, hist_ref, st_ref):
    xbc_raw = proj(O_XBC, O_U)
    dt_raw = proj(O_DT, D_IN_PACKED)
    rows = xbc_raw.shape[0]
    yield
    cat = jnp.concatenate([hist_ref[...], xbc_raw], axis=0)
    hist_ref[...] = xbc_raw[rows - SUBLANE:, :]
    acc = conv_b + conv_w[B_CONV - 1:B_CONV, :] * xbc_raw
    for n in (1, 2, 3):
        shifted = pltpu.roll(cat, n, axis=0)[SUBLANE:, :]
        acc = acc + conv_w[B_CONV - 1 - n:B_CONV - n, :] * shifted
    yield
    xbc = _silu(acc)
    xs = xbc[:, :B_W]
    bm = xbc[:, B_W:B_W + B_G * B_N].astype(BF16)
    cm = xbc[:, B_W + B_G * B_N:].astype(BF16)

    lane = lax.broadcasted_iota(jnp.int32, (1, LANE), 1)
    dt = jnp.where(pad_row, 0.0, _softplus(dt_raw + dt_bias))
    a_neg = jnp.where(lane < B_HEADS, -jnp.exp(a_log), 0.0)
    d_a = dt * (a_neg * LOG2_E)
    yield

    rr = lax.broadcasted_iota(jnp.int32, (rows, rows), 0)
    cc = lax.broadcasted_iota(jnp.int32, (rows, rows), 1)
    causal = cc <= rr
    tril = jnp.where(causal, 1.0, 0.0).astype(BF16)
    triu = jnp.where(rr <= cc, 1.0, 0.0).astype(BF16)
    d_a_parts = jnp.concatenate(_split_bf16(d_a, 3), axis=0)
    cum = _dot(jnp.concatenate([tril] * 3, axis=1), d_a_parts)
    cum_t = _dot_tn(d_a_parts, jnp.concatenate([triu] * 3, axis=0))
    yield

    def expand(x):
        return jnp.concatenate([jnp.broadcast_to(x[:, h:h + 1], (rows, B_P)) for h in range(B_HEADS)], axis=1)

    dt_full = expand(dt)
    cum_full = expand(cum)
    xdt = xs * dt_full
    lo = lane < 64
    cb = [_dot_nt(cm[:, B_N * g:B_N * (g + 1)], bm[:, B_N * g:B_N * (g + 1)]) for g in range(B_G)]
    yield

    y_parts = []
    for p in range(B_HEADS // 2):
        sc = []
        for h in (2 * p, 2 * p + 1):
            diff = cum[:, h:h + 1] - cum_t[h:h + 1, :]
            decay = jnp.exp2(jnp.where(causal, diff, -jnp.inf))
            sc.append((cb[h // (B_HEADS // B_G)] * decay).astype(BF16))
        xp = xdt[:, LANE * p:LANE * (p + 1)]
        zero = jnp.zeros_like(xp)
        xx = jnp.concatenate([jnp.where(lo, xp, zero), jnp.where(lo, zero, xp)], axis=0).astype(BF16)
        y_parts.append(_dot(jnp.concatenate(sc, axis=1), xx))
        yield
    y = jnp.concatenate(y_parts, axis=1)

    st = st_ref[...]
    y = y + _dot(cm, st.astype(BF16)) * jnp.exp2(cum_full) + d_full * xs
    cum_last = cum_full[rows - 1:rows, :]
    upd = _dot_tn(bm, (xdt * jnp.exp2(cum_last - cum_full)).astype(BF16))
    sr = lax.broadcasted_iota(jnp.int32, (B_G * B_N, B_W), 0)
    sc = lax.broadcasted_iota(jnp.int32, (B_G * B_N, B_W), 1)
    st_ref[...] = st * jnp.exp2(cum_last) + jnp.where((sr // B_N) == (sc // (B_W // B_G)), upd, 0.0)
    z_raw = proj(O_Z, O_XBC)
    yield

    y = y * _silu(z_raw)
    gr = lax.broadcasted_iota(jnp.int32, (B_W, B_W), 0)
    gc = lax.broadcasted_iota(jnp.int32, (B_W, B_W), 1)
    group_ones = jnp.where((gr // (B_W // B_G)) == (gc // (B_W // B_G)), 1.0, 0.0).astype(BF16)
    ms = _dot((y * y).astype(BF16), group_ones) * (1.0 / (B_W // B_G))
    return y * lax.rsqrt(ms + RMS_EPS) * norm_w


def _s5(proj, b_bd, c_bd, pow_re, pow_im, d_skip, glu_w, glu_b, carry_ref):
    u = proj(O_U, O_DT)
    rows = u.shape[0]
    groups = rows // SUBLANE
    sub = lax.broadcasted_iota(jnp.int32, (SUBLANE, 1), 0)
    u3 = u.reshape(groups, SUBLANE, C_W)
    delayed = [u.astype(BF16)] + [jnp.where(sub >= d, pltpu.roll(u3, d, axis=1), 0.0).reshape(rows, C_W).astype(BF16)
                                  for d in (1, 2, 3)]
    lo = lax.broadcasted_iota(jnp.int32, (1, LANE), 1) < 64
    quads = []
    for qd in range(C_G // 4):
        tile = slice(LANE * (qd // 2), LANE * (qd // 2 + 1))
        pieces = [x[:, tile] for x in delayed]
        if qd % 2 == 0:
            lhs = [jnp.where(lo, pieces[0], pltpu.roll(pieces[1], 64, axis=1)),
                   jnp.where(lo, pieces[2], pltpu.roll(pieces[3], 64, axis=1))]
        else:
            lhs = [jnp.where(lo, pltpu.roll(pieces[0], 64, axis=1), pieces[1]),
                   jnp.where(lo, pltpu.roll(pieces[2], 64, axis=1), pieces[3])]
        quads.append(_dot(jnp.concatenate(lhs, axis=1), b_bd[qd]))
    bu_re = jnp.concatenate([x[:, :4 * C_N] for x in quads], axis=1)
    bu_im = jnp.concatenate([x[:, 4 * C_N:] for x in quads], axis=1)
    yield
    x_re = bu_re.reshape(groups, SUBLANE, C_S)
    x_im = bu_im.reshape(groups, SUBLANE, C_S)
    for d in (4,):
        keep = sub >= d
        p_re = jnp.where(keep, pow_re[d - 1:d, :], 0.0)[None]
        p_im = jnp.where(keep, pow_im[d - 1:d, :], 0.0)[None]
        s_re = pltpu.roll(x_re, d, axis=1)
        s_im = pltpu.roll(x_im, d, axis=1)
        x_re, x_im = (x_re + (p_re * s_re - p_im * s_im), x_im + (p_re * s_im + p_im * s_re))
        yield

    c_re, c_im = carry_ref[0], carry_ref[1]
    g_re, g_im = [], []
    for i in range(groups):
        g_re.append(x_re[i] + (pow_re * c_re - pow_im * c_im))
        g_im.append(x_im[i] + (pow_re * c_im + pow_im * c_re))
        c_re, c_im = g_re[-1][SUBLANE - 1:, :], g_im[-1][SUBLANE - 1:, :]
        if i % 4 == 3:
            yield
    carry_ref[0] = c_re
    carry_ref[1] = c_im

    y = (_dot(jnp.concatenate(g_re, axis=0).astype(BF16), c_bd[:C_S, :])
         + _dot(jnp.concatenate(g_im, axis=0).astype(BF16), c_bd[C_S:, :]) + d_skip * u)
    yield
    y = jax.nn.gelu(y, approximate=True)
    gate = _dot(y.astype(BF16), glu_w) + glu_b
    yield
    return y * _sigmoid(gate)


def _interleave(gens):
    results = [None] * len(gens)
    live = list(range(len(gens)))
    while live:
        for i in list(live):
            try:
                next(gens[i])
            except StopIteration as stop:
                results[i] = stop.value
                live.remove(i)
    return results


def _mlp(prev, w1_ref, w2_ref, ln_g, ln_b, alpha):
    hb = prev.astype(BF16)
    ff = None
    for j in range(D_FF // MLP_CHUNK):
        cols = slice(j * MLP_CHUNK, (j + 1) * MLP_CHUNK)
        hid = jnp.maximum(_dot(hb, w1_ref[:, cols]), 0.0)
        yield
        part = _dot((hid * hid).astype(BF16), w2_ref[cols, :])
        ff = part if ff is None else ff + part
        yield
    return _layer_norm(alpha * prev + ff, ln_g, ln_b)


def _layer_kernel(alpha, front_pad, blocks_per_seq, h_ref, w_in_ref, log_lb_ref, one_m_lb_ref, a_norm_ref,
                  conv_w_ref, conv_b_ref, dt_bias_ref, a_log_ref, d_full_ref, b_norm_ref,
                  s5_b_ref, s5_c_ref, s5_pre_ref, s5_pim_ref, s5_d_ref, glu_w_ref, glu_b_ref,
                  w_out_ref, ln1_g_ref, ln1_b_ref, w1_ref, w2_ref, ln2_g_ref, ln2_b_ref, o_ref,
                  a_state, b_hist, b_state, c_carry, h1_prev):
    step = pl.program_id(0)
    blk = step % blocks_per_seq
    rows = h_ref.shape[0]

    @pl.when(step == 0)
    def _():
        h1_prev[...] = jnp.zeros_like(h1_prev)

    @pl.when(blk == 0)
    def _():
        a_state[...] = jnp.zeros_like(a_state)
        b_hist[...] = jnp.zeros_like(b_hist)
        b_state[...] = jnp.zeros_like(b_state)
        c_carry[...] = jnp.zeros_like(c_carry)

    row = lax.broadcasted_iota(jnp.int32, (rows, 1), 0) + blk * rows
    pad_row = row < front_pad
    h = h_ref[...]
    hb = jnp.where(pad_row, 0.0, h).astype(BF16)

    groups = [(O_Q, O_I), (O_U, D_IN_PACKED), (O_Z, O_U), (O_I, O_Z)]
    projected = [(lo_col, hi_col, _dot(hb, w_in_ref[:, lo_col:hi_col])) for lo_col, hi_col in groups]

    def proj(lo_col, hi_col):
        for g_lo, g_hi, arr in projected:
            if g_lo <= lo_col and hi_col <= g_hi:
                return arr[:, lo_col - g_lo:hi_col - g_lo]
        raise ValueError((lo_col, hi_col))

    gens = [_s5(proj, s5_b_ref[...], s5_c_ref[...], s5_pre_ref[...], s5_pim_ref[...],
                s5_d_ref[...], glu_w_ref[...], glu_b_ref[...], c_carry),
            _ssd(proj, conv_w_ref[...], conv_b_ref[...], dt_bias_ref[...], a_log_ref[...],
                 d_full_ref[...], b_norm_ref[...], pad_row, b_hist, b_state)]
    tx = lax.broadcasted_iota(jnp.int32, (rows, rows), 0) ^ lax.broadcasted_iota(jnp.int32, (rows, rows), 1)
    group_masks = (tx < 64, tx < 16, tx < 4)
    gens += [_hgrn2_pair(proj, p, log_lb_ref[...], one_m_lb_ref[...], a_norm_ref[...], pad_row, group_masks, a_state)
             for p in range(A_HEADS // 2)]
    gens.append(_mlp(h1_prev[...], w1_ref, w2_ref, ln2_g_ref[...], ln2_b_ref[...], alpha))
    y_c, y_b, *y_a, out_prev = _interleave(gens)
    o_ref[...] = out_prev
    y_all = jnp.concatenate([y.astype(BF16) for y in (*y_a, y_b, y_c)], axis=1)
    h1_prev[...] = _layer_norm(alpha * h + _dot(y_all, w_out_ref[...]), ln1_g_ref[...], ln1_b_ref[...])


def _layer_spec(arr, layer):
    zeros = (0,) * (arr.ndim - 1)
    return pl.BlockSpec((None,) + arr.shape[1:], lambda i: (layer,) + zeros, pipeline_mode=pl.Buffered(1))


def _layer_call(h2, params, layer, alpha, front_pad, blocks_per_seq):
    n, d = h2.shape
    rows = ROW_BLOCK
    n_blocks = n // rows
    assert n % rows == 0 and n_blocks % blocks_per_seq == 0
    return pl.pallas_call(
        functools.partial(_layer_kernel, alpha, front_pad, blocks_per_seq),
        grid=(n_blocks + 1,),
        in_specs=[pl.BlockSpec((rows, d), lambda i: (jnp.minimum(i, n_blocks - 1), 0))]
        + [_layer_spec(p, layer) for p in params],
        out_specs=pl.BlockSpec((rows, d), lambda i: (jnp.maximum(i - 1, 0), 0)),
        out_shape=jax.ShapeDtypeStruct(h2.shape, F32),
        scratch_shapes=[
            pltpu.VMEM((A_HEADS // 2, LANE, LANE), F32),
            pltpu.VMEM((SUBLANE, XBC_W), F32),
            pltpu.VMEM((B_G * B_N, B_W), F32),
            pltpu.VMEM((2, 1, C_S), F32),
            pltpu.VMEM((rows, d), F32),
        ],
        compiler_params=pltpu.CompilerParams(
            dimension_semantics=("arbitrary",), vmem_limit_bytes=VMEM_LIMIT),
        name="layer",
    )(h2, *params)


def _block_diag(blocks):
    g, r, c = blocks.shape
    tiled = jnp.tile(blocks.reshape(g * r, c), (1, g))
    rg = lax.broadcasted_iota(jnp.int32, (g * r, g * c), 0) // r
    cg = lax.broadcasted_iota(jnp.int32, (g * r, g * c), 1) // c
    return jnp.where(rg == cg, tiled, 0.0)


def _s5_params(a_re, a_im, log_dt, b_re, b_im, c_re, c_im):
    lam_re = jnp.minimum(a_re, S5_MAX_RE)
    lam_im = a_im
    dt = jnp.exp(log_dt)[:, None]
    mag = jnp.exp(lam_re * dt)
    lb_re = mag * jnp.cos(lam_im * dt)
    lb_im = mag * jnp.sin(lam_im * dt)
    den = jnp.square(lam_re) + jnp.square(lam_im)
    nr = lb_re - 1.0
    s_re = (nr * lam_re + lb_im * lam_im) / den
    s_im = (lb_im * lam_re - nr * lam_im) / den
    bb_re = s_re[..., None] * b_re - s_im[..., None] * b_im
    bb_im = s_re[..., None] * b_im + s_im[..., None] * b_re
    steps = jnp.arange(1, SUBLANE + 1, dtype=F32)[:, None, None]
    pmag = jnp.exp(lam_re * dt * steps)
    pw_re = pmag * jnp.cos(lam_im * dt * steps)
    pw_im = pmag * jnp.sin(lam_im * dt * steps)
    quads = C_G // 4
    b_rows = []
    for d in range(4):
        if d == 0:
            d_re, d_im = bb_re, bb_im
        else:
            pr, pi = pw_re[d - 1][..., None], pw_im[d - 1][..., None]
            d_re, d_im = bb_re * pr - bb_im * pi, bb_re * pi + bb_im * pr
        per_quad = lambda x: jax.vmap(_block_diag)(x.transpose(0, 2, 1).reshape(quads, 4, C_CH, C_N))
        b_rows.append(jnp.concatenate([per_quad(d_re), per_quad(d_im)], axis=2))
    b_bd = jnp.concatenate(b_rows, axis=1)
    c_bd = jnp.concatenate([_block_diag(c_re.transpose(0, 2, 1)), -_block_diag(c_im.transpose(0, 2, 1))], axis=0)
    return b_bd.astype(BF16), c_bd.astype(BF16), pw_re.reshape(SUBLANE, C_S), pw_im.reshape(SUBLANE, C_S)


def kernel(x, meta_tokens, w_in, hgrn_lb_logits, hgrn_norm_w, m2_conv_w, m2_conv_b, m2_dt_bias, m2_a_log, m2_d, m2_norm_w, s5_a_re, s5_a_im, s5_log_dt, s5_b_re, s5_b_im, s5_c_re, s5_c_im, s5_d, s5_glu_w, s5_glu_b, w_out, ln1_g, ln1_b, w_mlp_in, w_mlp_out, ln2_g, ln2_b):
    bsz, seq, d = x.shape
    depth = w_in.shape[0]
    alpha = (2 * depth) ** 0.25
    lp = -(-(N_META + seq) // ROW_BLOCK) * ROW_BLOCK
    front_pad = lp - N_META - seq

    meta = jnp.broadcast_to(meta_tokens.astype(x.dtype)[None], (bsz, N_META, d))
    hp = jnp.concatenate([jnp.zeros((bsz, front_pad, d), x.dtype), meta, x], axis=1)

    f32 = lambda v: v.astype(F32)
    row3 = lambda v: f32(v).reshape(depth, 1, -1)
    lb_cum = jnp.cumsum(jax.nn.softmax(f32(hgrn_lb_logits), axis=0), axis=0)
    lower = lb_cum - lb_cum[0]
    pad_cols = D_IN_PACKED - O_DT - B_HEADS
    w_in_p = jnp.concatenate([w_in[:, :, :2816], w_in[:, :, 2822:3078], w_in[:, :, 2816:2822],
                              jnp.zeros((depth, d, pad_cols), w_in.dtype)], axis=2).astype(BF16)
    b_bd, c_bd, pow_re, pow_im = jax.vmap(_s5_params)(
        f32(s5_a_re), f32(s5_a_im), f32(s5_log_dt), f32(s5_b_re), f32(s5_b_im), f32(s5_c_re), f32(s5_c_im))
    pad6 = lambda v: jnp.concatenate([f32(v), jnp.zeros((depth, LANE - B_HEADS), F32)], axis=1).reshape(depth, 1, LANE)
    params = [
        w_in_p, row3(jnp.log(lower)), row3(1.0 - lower), row3(jnp.tile(hgrn_norm_w, (1, A_HEADS))),
        f32(m2_conv_w), row3(m2_conv_b), pad6(m2_dt_bias), pad6(m2_a_log),
        row3(jnp.repeat(m2_d, B_P, axis=1)), row3(m2_norm_w),
        b_bd, c_bd, pow_re, pow_im, row3(s5_d), s5_glu_w.astype(BF16), row3(s5_glu_b),
        w_out.astype(BF16), row3(ln1_g), row3(ln1_b),
        w_mlp_in.astype(BF16), w_mlp_out.astype(BF16), row3(ln2_g), row3(ln2_b),
    ]

    h2 = hp.reshape(bsz * lp, d)
    for l in range(depth):
        h2 = _layer_call(h2, params, l, alpha, front_pad, lp // ROW_BLOCK)
    return h2.reshape(bsz, lp, d)[:, front_pad + N_META:]
```

```python
import functools

import jax
import jax.numpy as jnp
from jax import lax
from jax.experimental import pallas as pl
from jax.experimental.pallas import tpu as pltpu

F32 = jnp.float32
BF16 = jnp.bfloat16

D_MODEL = 1024
N_META = 16
A_HEADS, A_DK, A_W = 6, 64, 384
B_HEADS, B_P, B_W, B_G, B_N = 6, 64, 384, 2, 128
B_CONV = 4
C_G, C_CH, C_W, C_N = 16, 16, 256, 64
C_S = C_G * C_N
D_FF = 4 * D_MODEL
LN_EPS = 1e-5
RMS_EPS = 1e-6
S5_MAX_RE = -1e-4
LOG2_E = 1.4426950408889634

O_Q, O_F, O_I, O_G, O_Z, O_XBC, O_U, O_DT = 0, 384, 768, 1152, 1536, 1920, 2816, 3072
D_IN_PACKED = 3200
XBC_W = 896

ROW_BLOCK = 256
MLP_CHUNK = 512
LANE = 128
SUBLANE = 8
VMEM_LIMIT = 56 * 1024 * 1024


def _sigmoid(x):
    return 0.5 * jnp.tanh(0.5 * x) + 0.5


def _silu(x):
    hx = 0.5 * x
    return hx + hx * jnp.tanh(hx)


def _log1p_exp_neg_abs(x):
    return jnp.log(1.0 + jnp.exp(-jnp.abs(x)))


def _softplus(x):
    return jnp.maximum(x, 0.0) + _log1p_exp_neg_abs(x)


def _split_bf16(x, parts):
    out = []
    r = x
    for i in range(parts):
        p = r.astype(BF16)
        out.append(p)
        if i + 1 < parts:
            r = r - p.astype(F32)
    return out


def _dot(a, b):
    return jnp.dot(a, b, preferred_element_type=F32)


def _dot_nt(a, b):
    return lax.dot_general(a, b, (((1,), (1,)), ((), ())), preferred_element_type=F32)


def _dot_tn(a, b):
    return lax.dot_general(a, b, (((0,), (0,)), ((), ())), preferred_element_type=F32)


def _shift_rows(x, n):
    rows = x.shape[0]
    if n == 0:
        return x
    if n % SUBLANE == 0:
        z = jnp.zeros((abs(n), x.shape[1]), x.dtype)
        if n > 0:
            return jnp.concatenate([z, x[: rows - n]], axis=0)
        return jnp.concatenate([x[-n:], z], axis=0)
    return pltpu.roll(x, n % rows, axis=0)


def _layer_norm(x, g, b):
    mu = jnp.mean(x, axis=-1, keepdims=True)
    xc = x - mu
    var = jnp.mean(xc * xc, axis=-1, keepdims=True)
    return xc * lax.rsqrt(var + LN_EPS) * g + b


def _hgrn2_pair(proj, pair, log_lb, one_m_lb, norm_w, pad_row, group_masks, st_ref):
    sl = slice(LANE * pair, LANE * (pair + 1))
    col = lambda off: proj(off + LANE * pair, off + LANE * (pair + 1))
    f_raw = col(O_F)
    q_raw = col(O_Q)
    rows = f_raw.shape[0]
    assert rows == 256
    yield
    q = _silu(q_raw)
    ls_pos = jnp.minimum(f_raw, 0.0) - _log1p_exp_neg_abs(f_raw)
    ls_neg = ls_pos - f_raw
    gap = f_raw - log_lb[:, sl]
    lf = ls_pos + jnp.maximum(-gap, 0.0) + _log1p_exp_neg_abs(gap)
    lf = jnp.where(pad_row, 0.0, lf)
    k = jnp.where(pad_row, 0.0, one_m_lb[:, sl] * jnp.exp(ls_neg))
    yield

    row = lax.broadcasted_iota(jnp.int32, (rows, 1), 0)
    lane = lax.broadcasted_iota(jnp.int32, (1, LANE), 1)
    lo = lane < 64
    zero_h = jnp.zeros((rows, LANE), BF16)

    def pack_heads(s1, s2, s3):
        s1r = pltpu.roll(s1, 64, axis=1)
        s2r = pltpu.roll(s2, 64, axis=1)
        even = jnp.concatenate([jnp.where(lo, s1, s2r), jnp.where(lo, s3, zero_h)], axis=1)
        odd = jnp.concatenate([jnp.where(lo, s1r, s2), jnp.where(lo, zero_h, s3)], axis=1)
        return even, odd

    lf = lf * LOG2_E
    cs, sf, tot = lf, jnp.zeros_like(lf), lf
    scores = {}
    for c in (1, 4, 16, 64):
        j = (row // c) % 4
        qt = (q * jnp.exp2(cs)).astype(BF16)
        up1, up2 = _shift_rows(tot, -c), _shift_rows(tot, -2 * c)
        e2 = sf + up1
        e3 = e2 + up2
        ke1, ke2, ke3 = [(k * jnp.exp2(e)).astype(BF16) for e in (sf, e2, e3)]
        qa, qb = pack_heads(*[jnp.where(j == i, qt, zero_h) for i in (1, 2, 3)])
        ka, kb = pack_heads(jnp.where(j == 0, ke1, zero_h),
                            jnp.where(j == 1, ke1, jnp.where(j == 0, ke2, zero_h)),
                            jnp.where(j == 2, ke1, jnp.where(j == 1, ke2, jnp.where(j == 0, ke3, zero_h))))
        scores[c] = (_dot_nt(qa, ka), _dot_nt(qb, kb))
        yield
        new_cs = cs + (jnp.where(j >= 1, _shift_rows(tot, c), 0.0) + jnp.where(j >= 2, _shift_rows(tot, 2 * c), 0.0)
                       + jnp.where(j >= 3, _shift_rows(tot, 3 * c), 0.0))
        new_sf = sf + (jnp.where(j <= 2, up1, 0.0) + jnp.where(j <= 1, up2, 0.0)
                       + jnp.where(j <= 0, _shift_rows(tot, -3 * c), 0.0))
        cs, sf = new_cs, new_sf
        tot = cs + sf
        yield

    q_in = (q * jnp.exp2(cs)).astype(BF16)
    k_out = (k * jnp.exp2(sf)).astype(BF16)
    decay_blk = jnp.exp2(tot[0:1, :])
    v_raw = col(O_I)
    vp = v_raw.astype(BF16)
    zero_b = jnp.zeros_like(vp)
    r2 = lax.broadcasted_iota(jnp.int32, (LANE, LANE), 0)
    c2 = lax.broadcasted_iota(jnp.int32, (LANE, LANE), 1)
    head_diag = (r2 // 64) == (c2 // 64)
    head_ones = jnp.where(head_diag, 1.0, 0.0).astype(BF16)
    o_diag = _dot((q * k).astype(BF16), head_ones) * vp.astype(F32)
    yield

    same64, same16, same4 = group_masks
    sc = []
    for hh in range(2):
        sc.append(jnp.where(same4, scores[1][hh],
                            jnp.where(same16, scores[4][hh],
                                      jnp.where(same64, scores[16][hh], scores[64][hh]))).astype(BF16))
        yield
    st = st_ref[pair]
    vv = jnp.concatenate([jnp.where(lo, vp, zero_b), jnp.where(lo, zero_b, vp)], axis=0)
    o = _dot(jnp.concatenate(sc, axis=1), vv) + _dot_nt(q_in, st.astype(BF16)) + o_diag
    upd = _dot_tn(vp, k_out)
    st_ref[pair] = st * decay_blk + jnp.where(head_diag, upd, 0.0)
    g_raw = col(O_G)
    yield

    ms = _dot((o * o).astype(BF16), head_ones) * (1.0 / A_DK)
    return o * lax.rsqrt(ms + RMS_EPS) * norm_w[:, sl] * _silu(g_raw)


def _ssd(proj, conv_w, conv_b, dt_bias, a_log, d_full, norm_w, pad_row, hist_ref, st_ref):
    xbc_raw = proj(O_XBC, O_U)
    dt_raw = proj(O_DT, D_IN_PACKED)
    rows = xbc_raw.shape[0]
    yield
    cat = jnp.concatenate([hist_ref[...], xbc_raw], axis=0)
    hist_ref[...] = xbc_raw[rows - SUBLANE:, :]
    acc = conv_b + conv_w[B_CONV - 1:B_CONV, :] * xbc_raw
    for n in (1, 2, 3):
        shifted = pltpu.roll(cat, n, axis=0)[SUBLANE:, :]
        acc = acc + conv_w[B_CONV - 1 - n:B_CONV - n, :] * shifted
    yield
    xbc = _silu(acc)
    xs = xbc[:, :B_W]
    bm = xbc[:, B_W:B_W + B_G * B_N].astype(BF16)
    cm = xbc[:, B_W + B_G * B_N:].astype(BF16)

    lane = lax.broadcasted_iota(jnp.int32, (1, LANE), 1)
    dt = jnp.where(pad_row, 0.0, _softplus(dt_raw + dt_bias))
    a_neg = jnp.where(lane < B_HEADS, -jnp.exp(a_log), 0.0)
    d_a = dt * (a_neg * LOG2_E)
    yield

    rr = lax.broadcasted_iota(jnp.int32, (rows, rows), 0)
    cc = lax.broadcasted_iota(jnp.int32, (rows, rows), 1)
    causal = cc <= rr
    tril = jnp.where(causal, 1.0, 0.0).astype(BF16)
    triu = jnp.where(rr <= cc, 1.0, 0.0).astype(BF16)
    d_a_parts = jnp.concatenate(_split_bf16(d_a, 3), axis=0)
    cum = _dot(jnp.concatenate([tril] * 3, axis=1), d_a_parts)
    cum_t = _dot_tn(d_a_parts, jnp.concatenate([triu] * 3, axis=0))
    yield

    def expand(x):
        return jnp.concatenate([jnp.broadcast_to(x[:, h:h + 1], (rows, B_P)) for h in range(B_HEADS)], axis=1)

    dt_full = expand(dt)
    cum_full = expand(cum)
    xdt = xs * dt_full
    lo = lane < 64
    cb = [_dot_nt(cm[:, B_N * g:B_N * (g + 1)], bm[:, B_N * g:B_N * (g + 1)]) for g in range(B_G)]
    yield

    y_parts = []
    for p in range(B_HEADS // 2):
        sc = []
        for h in (2 * p, 2 * p + 1):
            diff = cum[:, h:h + 1] - cum_t[h:h + 1, :]
            decay = jnp.exp2(jnp.where(causal, diff, -jnp.inf))
            sc.append((cb[h // (B_HEADS // B_G)] * decay).astype(BF16))
        xp = xdt[:, LANE * p:LANE * (p + 1)]
        zero = jnp.zeros_like(xp)
        xx = jnp.concatenate([jnp.where(lo, xp, zero), jnp.where(lo, zero, xp)], axis=0).astype(BF16)
        y_parts.append(_dot(jnp.concatenate(sc, axis=1), xx))
        yield
    y = jnp.concatenate(y_parts, axis=1)

    st = st_ref[...]
    y = y + _dot(cm, st.astype(BF16)) * jnp.exp2(cum_full) + d_full * xs
    cum_last = cum_full[rows - 1:rows, :]
    upd = _dot_tn(bm, (xdt * jnp.exp2(cum_last - cum_full)).astype(BF16))
    sr = lax.broadcasted_iota(jnp.int32, (B_G * B_N, B_W), 0)
    sc = lax.broadcasted_iota(jnp.int32, (B_G * B_N, B_W), 1)
    st_ref[...] = st * jnp.exp2(cum_last) + jnp.where((sr // B_N) == (sc // (B_W // B_G)), upd, 0.0)
    z_raw = proj(O_Z, O_XBC)
    yield

    y = y * _silu(z_raw)
    gr = lax.broadcasted_iota(jnp.int32, (B_W, B_W), 0)
    gc = lax.broadcasted_iota(jnp.int32, (B_W, B_W), 1)
    group_ones = jnp.where((gr // (B_W // B_G)) == (gc // (B_W // B_G)), 1.0, 0.0).astype(BF16)
    ms = _dot((y * y).astype(BF16), group_ones) * (1.0 / (B_W // B_G))
    return y * lax.rsqrt(ms + RMS_EPS) * norm_w


def _s5(proj, b_bd, c_bd, pow_re, pow_im, d_skip, glu_w, glu_b, carry_ref):
    u = proj(O_U, O_DT)
    rows = u.shape[0]
    groups = rows // SUBLANE
    sub = lax.broadcasted_iota(jnp.int32, (SUBLANE, 1), 0)
    u3 = u.reshape(groups, SUBLANE, C_W)
    delayed = [u.astype(BF16)] + [jnp.where(sub >= d, pltpu.roll(u3, d, axis=1), 0.0).reshape(rows, C_W).astype(BF16)
                                  for d in (1, 2, 3)]
    lo = lax.broadcasted_iota(jnp.int32, (1, LANE), 1) < 64
    quads = []
    for qd in range(C_G // 4):
        tile = slice(LANE * (qd // 2), LANE * (qd // 2 + 1))
        pieces = [x[:, tile] for x in delayed]
        if qd % 2 == 0:
            lhs = [jnp.where(lo, pieces[0], pltpu.roll(pieces[1], 64, axis=1)),
                   jnp.where(lo, pieces[2], pltpu.roll(pieces[3], 64, axis=1))]
        else:
            lhs = [jnp.where(lo, pltpu.roll(pieces[0], 64, axis=1), pieces[1]),
                   jnp.where(lo, pltpu.roll(pieces[2], 64, axis=1), pieces[3])]
        quads.append(_dot(jnp.concatenate(lhs, axis=1), b_bd[qd]))
    bu_re = jnp.concatenate([x[:, :4 * C_N] for x in quads], axis=1)
    bu_im = jnp.concatenate([x[:, 4 * C_N:] for x in quads], axis=1)
    yield
    x_re = bu_re.reshape(groups, SUBLANE, C_S)
    x_im = bu_im.reshape(groups, SUBLANE, C_S)
    for d in (4,):
        keep = sub >= d
        p_re = jnp.where(keep, pow_re[d - 1:d, :], 0.0)[None]
        p_im = jnp.where(keep, pow_im[d - 1:d, :], 0.0)[None]
        s_re = pltpu.roll(x_re, d, axis=1)
        s_im = pltpu.roll(x_im, d, axis=1)
        x_re, x_im = (x_re + (p_re * s_re - p_im * s_im), x_im + (p_re * s_im + p_im * s_re))
        yield

    c_re, c_im = carry_ref[0], carry_ref[1]
    g_re, g_im = [], []
    for i in range(groups):
        g_re.append(x_re[i] + (pow_re * c_re - pow_im * c_im))
        g_im.append(x_im[i] + (pow_re * c_im + pow_im * c_re))
        c_re, c_im = g_re[-1][SUBLANE - 1:, :], g_im[-1][SUBLANE - 1:, :]
        if i % 4 == 3:
            yield
    carry_ref[0] = c_re
    carry_ref[1] = c_im

    y = (_dot(jnp.concatenate(g_re, axis=0).astype(BF16), c_bd[:C_S, :])
         + _dot(jnp.concatenate(g_im, axis=0).astype(BF16), c_bd[C_S:, :]) + d_skip * u)
    yield
    y = jax.nn.gelu(y, approximate=True)
    gate = _dot(y.astype(BF16), glu_w) + glu_b
    yield
    return y * _sigmoid(gate)


def _interleave(gens):
    results = [None] * len(gens)
    live = list(range(len(gens)))
    while live:
        for i in list(live):
            try:
                next(gens[i])
            except StopIteration as stop:
                results[i] = stop.value
                live.remove(i)
    return results


def _mlp(prev, w1_ref, w2_ref, ln_g, ln_b, alpha):
    hb = prev.astype(BF16)
    ff = None
    for j in range(D_FF // MLP_CHUNK):
        cols = slice(j * MLP_CHUNK, (j + 1) * MLP_CHUNK)
        hid = jnp.maximum(_dot(hb, w1_ref[:, cols]), 0.0)
        yield
        part = _dot((hid * hid).astype(BF16), w2_ref[cols, :])
        ff = part if ff is None else ff + part
        yield
    return _layer_norm(alpha * prev + ff, ln_g, ln_b)


def _layer_kernel(alpha, front_pad, blocks_per_seq, first, h_ref, front_ref, w_in_ref, log_lb_ref, one_m_lb_ref, a_norm_ref,
                  conv_w_ref, conv_b_ref, dt_bias_ref, a_log_ref, d_full_ref, b_norm_ref,
                  s5_b_ref, s5_c_ref, s5_pre_ref, s5_pim_ref, s5_d_ref, glu_w_ref, glu_b_ref,
                  w_out_ref, ln1_g_ref, ln1_b_ref, w1_ref, w2_ref, ln2_g_ref, ln2_b_ref, o_ref,
                  a_state, b_hist, b_state, c_carry, h1_prev):
    step = pl.program_id(0)
    blk = step % blocks_per_seq
    rows = h_ref.shape[0]

    @pl.when(step == 0)
    def _():
        h1_prev[...] = jnp.zeros_like(h1_prev)

    @pl.when(blk == 0)
    def _():
        a_state[...] = jnp.zeros_like(a_state)
        b_hist[...] = jnp.zeros_like(b_hist)
        b_state[...] = jnp.zeros_like(b_state)
        c_carry[...] = jnp.zeros_like(c_carry)

    row = lax.broadcasted_iota(jnp.int32, (rows, 1), 0) + blk * rows
    pad_row = row < front_pad
    h = h_ref[...]
    if first:
        h = jnp.where(blk == 0, front_ref[...], h)
    hb = jnp.where(pad_row, 0.0, h).astype(BF16)

    groups = [(O_Q, O_I), (O_U, D_IN_PACKED), (O_Z, O_U), (O_I, O_Z)]
    projected = [(lo_col, hi_col, _dot(hb, w_in_ref[:, lo_col:hi_col])) for lo_col, hi_col in groups]

    def proj(lo_col, hi_col):
        for g_lo, g_hi, arr in projected:
            if g_lo <= lo_col and hi_col <= g_hi:
                return arr[:, lo_col - g_lo:hi_col - g_lo]
        raise ValueError((lo_col, hi_col))

    gens = [_s5(proj, s5_b_ref[...], s5_c_ref[...], s5_pre_ref[...], s5_pim_ref[...],
                s5_d_ref[...], glu_w_ref[...], glu_b_ref[...], c_carry),
            _ssd(proj, conv_w_ref[...], conv_b_ref[...], dt_bias_ref[...], a_log_ref[...],
                 d_full_ref[...], b_norm_ref[...], pad_row, b_hist, b_state)]
    tx = lax.broadcasted_iota(jnp.int32, (rows, rows), 0) ^ lax.broadcasted_iota(jnp.int32, (rows, rows), 1)
    group_masks = (tx < 64, tx < 16, tx < 4)
    gens += [_hgrn2_pair(proj, p, log_lb_ref[...], one_m_lb_ref[...], a_norm_ref[...], pad_row, group_masks, a_state)
             for p in range(A_HEADS // 2)]
    gens.append(_mlp(h1_prev[...], w1_ref, w2_ref, ln2_g_ref[...], ln2_b_ref[...], alpha))
    y_c, y_b, *y_a, out_prev = _interleave(gens)
    o_ref[...] = out_prev
    y_all = jnp.concatenate([y.astype(BF16) for y in (*y_a, y_b, y_c)], axis=1)
    h1_prev[...] = _layer_norm(alpha * h + _dot(y_all, w_out_ref[...]), ln1_g_ref[...], ln1_b_ref[...])


def _layer_spec(arr, layer):
    zeros = (0,) * (arr.ndim - 1)
    return pl.BlockSpec((None,) + arr.shape[1:], lambda i: (layer,) + zeros, pipeline_mode=pl.Buffered(1))


def _layer_call(h_in, front, params, layer, alpha, front_pad, blocks_per_seq, n_seq, first, last):
    rows = ROW_BLOCK
    d = h_in.shape[1]
    n_blocks = n_seq * blocks_per_seq

    def unpadded(m):
        return (m // blocks_per_seq) * (blocks_per_seq - 1) + jnp.maximum(m % blocks_per_seq - 1, 0)

    def in_map(i):
        m = jnp.minimum(i, n_blocks - 1)
        return (unpadded(m) if first else m, 0)

    def out_map(i):
        m = jnp.maximum(i - 1, 0)
        return (unpadded(m) if last else m, 0)

    out_rows = (n_blocks - n_seq if last else n_blocks) * rows
    assert h_in.shape[0] == (n_blocks - n_seq if first else n_blocks) * rows
    return pl.pallas_call(
        functools.partial(_layer_kernel, alpha, front_pad, blocks_per_seq, first),
        grid=(n_blocks + 1,),
        in_specs=[pl.BlockSpec((rows, d), in_map), pl.BlockSpec((rows, d), lambda i: (0, 0))]
        + [_layer_spec(p, layer) for p in params],
        out_specs=pl.BlockSpec((rows, d), out_map),
        out_shape=jax.ShapeDtypeStruct((out_rows, d), F32),
        scratch_shapes=[
            pltpu.VMEM((A_HEADS // 2, LANE, LANE), F32),
            pltpu.VMEM((SUBLANE, XBC_W), F32),
            pltpu.VMEM((B_G * B_N, B_W), F32),
            pltpu.VMEM((2, 1, C_S), F32),
            pltpu.VMEM((rows, d), F32),
        ],
        compiler_params=pltpu.CompilerParams(
            dimension_semantics=("arbitrary",), vmem_limit_bytes=VMEM_LIMIT),
        name="layer",
    )(h_in, front, *params)


def _block_diag(blocks):
    g, r, c = blocks.shape
    tiled = jnp.tile(blocks.reshape(g * r, c), (1, g))
    rg = lax.broadcasted_iota(jnp.int32, (g * r, g * c), 0) // r
    cg = lax.broadcasted_iota(jnp.int32, (g * r, g * c), 1) // c
    return jnp.where(rg == cg, tiled, 0.0)


def _s5_params(a_re, a_im, log_dt, b_re, b_im, c_re, c_im):
    lam_re = jnp.minimum(a_re, S5_MAX_RE)
    lam_im = a_im
    dt = jnp.exp(log_dt)[:, None]
    mag = jnp.exp(lam_re * dt)
    lb_re = mag * jnp.cos(lam_im * dt)
    lb_im = mag * jnp.sin(lam_im * dt)
    den = jnp.square(lam_re) + jnp.square(lam_im)
    nr = lb_re - 1.0
    s_re = (nr * lam_re + lb_im * lam_im) / den
    s_im = (lb_im * lam_re - nr * lam_im) / den
    bb_re = s_re[..., None] * b_re - s_im[..., None] * b_im
    bb_im = s_re[..., None] * b_im + s_im[..., None] * b_re
    steps = jnp.arange(1, SUBLANE + 1, dtype=F32)[:, None, None]
    pmag = jnp.exp(lam_re * dt * steps)
    pw_re = pmag * jnp.cos(lam_im * dt * steps)
    pw_im = pmag * jnp.sin(lam_im * dt * steps)
    quads = C_G // 4
    b_rows = []
    for d in range(4):
        if d == 0:
            d_re, d_im = bb_re, bb_im
        else:
            pr, pi = pw_re[d - 1][..., None], pw_im[d - 1][..., None]
            d_re, d_im = bb_re * pr - bb_im * pi, bb_re * pi + bb_im * pr
        per_quad = lambda x: jax.vmap(_block_diag)(x.transpose(0, 2, 1).reshape(quads, 4, C_CH, C_N))
        b_rows.append(jnp.concatenate([per_quad(d_re), per_quad(d_im)], axis=2))
    b_bd = jnp.concatenate(b_rows, axis=1)
    c_bd = jnp.concatenate([_block_diag(c_re.transpose(0, 2, 1)), -_block_diag(c_im.transpose(0, 2, 1))], axis=0)
    return b_bd.astype(BF16), c_bd.astype(BF16), pw_re.reshape(SUBLANE, C_S), pw_im.reshape(SUBLANE, C_S)


def kernel(x, meta_tokens, w_in, hgrn_lb_logits, hgrn_norm_w, m2_conv_w, m2_conv_b, m2_dt_bias, m2_a_log, m2_d, m2_norm_w, s5_a_re, s5_a_im, s5_log_dt, s5_b_re, s5_b_im, s5_c_re, s5_c_im, s5_d, s5_glu_w, s5_glu_b, w_out, ln1_g, ln1_b, w_mlp_in, w_mlp_out, ln2_g, ln2_b):
    bsz, seq, d = x.shape
    depth = w_in.shape[0]
    alpha = (2 * depth) ** 0.25
    assert seq % ROW_BLOCK == 0
    front_pad = ROW_BLOCK - N_META
    front = jnp.concatenate([jnp.zeros((front_pad, d), x.dtype), meta_tokens.astype(x.dtype)], axis=0)

    f32 = lambda v: v.astype(F32)
    row3 = lambda v: f32(v).reshape(depth, 1, -1)
    lb_cum = jnp.cumsum(jax.nn.softmax(f32(hgrn_lb_logits), axis=0), axis=0)
    lower = lb_cum - lb_cum[0]
    pad_cols = D_IN_PACKED - O_DT - B_HEADS
    w_in_p = jnp.concatenate([w_in[:, :, :2816], w_in[:, :, 2822:3078], w_in[:, :, 2816:2822],
                              jnp.zeros((depth, d, pad_cols), w_in.dtype)], axis=2).astype(BF16)
    b_bd, c_bd, pow_re, pow_im = jax.vmap(_s5_params)(
        f32(s5_a_re), f32(s5_a_im), f32(s5_log_dt), f32(s5_b_re), f32(s5_b_im), f32(s5_c_re), f32(s5_c_im))
    pad6 = lambda v: jnp.concatenate([f32(v), jnp.zeros((depth, LANE - B_HEADS), F32)], axis=1).reshape(depth, 1, LANE)
    params = [
        w_in_p, row3(jnp.log(lower)), row3(1.0 - lower), row3(jnp.tile(hgrn_norm_w, (1, A_HEADS))),
        f32(m2_conv_w), row3(m2_conv_b), pad6(m2_dt_bias), pad6(m2_a_log),
        row3(jnp.repeat(m2_d, B_P, axis=1)), row3(m2_norm_w),
        b_bd, c_bd, pow_re, pow_im, row3(s5_d), s5_glu_w.astype(BF16), row3(s5_glu_b),
        w_out.astype(BF16), row3(ln1_g), row3(ln1_b),
        w_mlp_in.astype(BF16), w_mlp_out.astype(BF16), row3(ln2_g), row3(ln2_b),
    ]

    h2 = x.reshape(bsz * seq, d)
    for l in range(depth):
        h2 = _layer_call(h2, front, params, l, alpha, front_pad, seq // ROW_BLOCK + 1, bsz,
                         first=l == 0, last=l == depth - 1)
    return h2.reshape(bsz, seq, d)
```

```python
import functools

import jax
import jax.numpy as jnp
from jax import lax
from jax.experimental import pallas as pl
from jax.experimental.pallas import tpu as pltpu

F32 = jnp.float32
BF16 = jnp.bfloat16

D_MODEL = 1024
N_META = 16
A_HEADS, A_DK, A_W = 6, 64, 384
B_HEADS, B_P, B_W, B_G, B_N = 6, 64, 384, 2, 128
B_CONV = 4
C_G, C_CH, C_W, C_N = 16, 16, 256, 64
C_S = C_G * C_N
QUAD_W = 4 * C_CH
S5_FOLD = 4
D_FF = 4 * D_MODEL
LN_EPS = 1e-5
RMS_EPS = 1e-6
S5_MAX_RE = -1e-4
LOG2_E = 1.4426950408889634

O_Q, O_F, O_I, O_G, O_Z, O_XBC, O_U, O_DT = 0, 384, 768, 1152, 1536, 1920, 2816, 3072
D_IN_PACKED = 3200
XBC_W = 896

HGRN_RADIX = 4
HGRN_LEVELS = (1, 4, 16, 64)
ROW_BLOCK = 256
MLP_CHUNK = 512
S5_START_ROUND, SSD_START_ROUND = 3, 1
LANE = 128
SUBLANE = 8
VMEM_LIMIT = 56 * 1024 * 1024


def _sigmoid(x):
    return 0.5 * jnp.tanh(0.5 * x) + 0.5


def _silu(x):
    hx = 0.5 * x
    return hx + hx * jnp.tanh(hx)


def _log1p_exp_neg_abs(x):
    return jnp.log(1.0 + jnp.exp(-jnp.abs(x)))


def _softplus(x):
    return jnp.maximum(x, 0.0) + _log1p_exp_neg_abs(x)


def _split_bf16(x, parts):
    out = []
    r = x
    for i in range(parts):
        p = r.astype(BF16)
        out.append(p)
        if i + 1 < parts:
            r = r - p.astype(F32)
    return out


def _dot(a, b):
    return jnp.dot(a, b, preferred_element_type=F32)


def _dot_nt(a, b):
    return lax.dot_general(a, b, (((1,), (1,)), ((), ())), preferred_element_type=F32)


def _dot_tn(a, b):
    return lax.dot_general(a, b, (((0,), (0,)), ((), ())), preferred_element_type=F32)


def _shift_rows(x, n):
    rows = x.shape[0]
    if n == 0:
        return x
    if n % SUBLANE == 0:
        z = jnp.zeros((abs(n), x.shape[1]), x.dtype)
        if n > 0:
            return jnp.concatenate([z, x[: rows - n]], axis=0)
        return jnp.concatenate([x[-n:], z], axis=0)
    return pltpu.roll(x, n % rows, axis=0)


def _layer_norm(x, g, b):
    mu = jnp.mean(x, axis=-1, keepdims=True)
    xc = x - mu
    var = jnp.mean(xc * xc, axis=-1, keepdims=True)
    return xc * lax.rsqrt(var + LN_EPS) * g + b


def _hgrn2_pair(proj, pair, log_lb, one_m_lb, norm_w, pad_row, group_masks, st_ref):
    sl = slice(LANE * pair, LANE * (pair + 1))
    col = lambda off: proj(off + LANE * pair, off + LANE * (pair + 1))
    f_raw = col(O_F)
    q_raw = col(O_Q)
    rows = f_raw.shape[0]
    assert rows == HGRN_RADIX * HGRN_LEVELS[-1] and HGRN_RADIX == 4
    yield
    q = _silu(q_raw)
    ls_pos = jnp.minimum(f_raw, 0.0) - _log1p_exp_neg_abs(f_raw)
    ls_neg = ls_pos - f_raw
    gap = f_raw - log_lb[:, sl]
    lf = ls_pos + jnp.maximum(-gap, 0.0) + _log1p_exp_neg_abs(gap)
    lf = jnp.where(pad_row, 0.0, lf)
    k = jnp.where(pad_row, 0.0, one_m_lb[:, sl] * jnp.exp(ls_neg))
    yield

    row = lax.broadcasted_iota(jnp.int32, (rows, 1), 0)
    lane = lax.broadcasted_iota(jnp.int32, (1, LANE), 1)
    lo = lane < A_DK
    zero_h = jnp.zeros((rows, LANE), BF16)

    def pack_heads(s1, s2, s3):
        s1r = pltpu.roll(s1, A_DK, axis=1)
        s2r = pltpu.roll(s2, A_DK, axis=1)
        even = jnp.concatenate([jnp.where(lo, s1, s2r), jnp.where(lo, s3, zero_h)], axis=1)
        odd = jnp.concatenate([jnp.where(lo, s1r, s2), jnp.where(lo, zero_h, s3)], axis=1)
        return even, odd

    lf = lf * LOG2_E
    cs, sf, tot = lf, jnp.zeros_like(lf), lf
    scores = {}
    for c in HGRN_LEVELS:
        j = (row // c) % HGRN_RADIX
        qt = (q * jnp.exp2(cs)).astype(BF16)
        up1, up2 = _shift_rows(tot, -c), _shift_rows(tot, -2 * c)
        e2 = sf + up1
        e3 = e2 + up2
        ke1, ke2, ke3 = [(k * jnp.exp2(e)).astype(BF16) for e in (sf, e2, e3)]
        qa, qb = pack_heads(*[jnp.where(j == i, qt, zero_h) for i in (1, 2, 3)])
        ka, kb = pack_heads(jnp.where(j == 0, ke1, zero_h),
                            jnp.where(j == 1, ke1, jnp.where(j == 0, ke2, zero_h)),
                            jnp.where(j == 2, ke1, jnp.where(j == 1, ke2, jnp.where(j == 0, ke3, zero_h))))
        scores[c] = (_dot_nt(qa, ka), _dot_nt(qb, kb))
        yield
        new_cs = cs + (jnp.where(j >= 1, _shift_rows(tot, c), 0.0) + jnp.where(j >= 2, _shift_rows(tot, 2 * c), 0.0)
                       + jnp.where(j >= 3, _shift_rows(tot, 3 * c), 0.0))
        new_sf = sf + (jnp.where(j <= 2, up1, 0.0) + jnp.where(j <= 1, up2, 0.0)
                       + jnp.where(j <= 0, _shift_rows(tot, -3 * c), 0.0))
        cs, sf = new_cs, new_sf
        tot = cs + sf
        yield

    q_in = (q * jnp.exp2(cs)).astype(BF16)
    k_out = (k * jnp.exp2(sf)).astype(BF16)
    decay_blk = jnp.exp2(tot[0:1, :])
    v_raw = col(O_I)
    vp = v_raw.astype(BF16)
    zero_b = jnp.zeros_like(vp)
    r2 = lax.broadcasted_iota(jnp.int32, (LANE, LANE), 0)
    c2 = lax.broadcasted_iota(jnp.int32, (LANE, LANE), 1)
    head_diag = (r2 // A_DK) == (c2 // A_DK)
    head_ones = jnp.where(head_diag, 1.0, 0.0).astype(BF16)
    o_diag = _dot((q * k).astype(BF16), head_ones) * vp.astype(F32)
    yield

    same64, same16, same4 = group_masks
    sc = []
    for hh in range(2):
        sc.append(jnp.where(same4, scores[1][hh],
                            jnp.where(same16, scores[4][hh],
                                      jnp.where(same64, scores[16][hh], scores[64][hh]))).astype(BF16))
        yield
    st = st_ref[pair]
    vv = jnp.concatenate([jnp.where(lo, vp, zero_b), jnp.where(lo, zero_b, vp)], axis=0)
    o = _dot(jnp.concatenate(sc, axis=1), vv) + _dot_nt(q_in, st.astype(BF16)) + o_diag
    upd = _dot_tn(vp, k_out)
    st_ref[pair] = st * decay_blk + jnp.where(head_diag, upd, 0.0)
    g_raw = col(O_G)
    yield

    ms = _dot((o * o).astype(BF16), head_ones) * (1.0 / A_DK)
    return o * lax.rsqrt(ms + RMS_EPS) * norm_w[:, sl] * _silu(g_raw)


def _ssd(proj, conv_w, conv_b, dt_bias, a_log, d_full, norm_w, pad_row, hist_ref, st_ref):
    xbc_raw = proj(O_XBC, O_U)
    dt_raw = proj(O_DT, D_IN_PACKED)
    rows = xbc_raw.shape[0]
    yield
    cat = jnp.concatenate([hist_ref[...], xbc_raw], axis=0)
    hist_ref[...] = xbc_raw[rows - SUBLANE:, :]
    acc = conv_b + conv_w[B_CONV - 1:B_CONV, :] * xbc_raw
    for n in (1, 2, 3):
        shifted = pltpu.roll(cat, n, axis=0)[SUBLANE:, :]
        acc = acc + conv_w[B_CONV - 1 - n:B_CONV - n, :] * shifted
    yield
    xbc = _silu(acc)
    xs = xbc[:, :B_W]
    bm = xbc[:, B_W:B_W + B_G * B_N].astype(BF16)
    cm = xbc[:, B_W + B_G * B_N:].astype(BF16)

    lane = lax.broadcasted_iota(jnp.int32, (1, LANE), 1)
    dt = jnp.where(pad_row, 0.0, _softplus(dt_raw + dt_bias))
    a_neg = jnp.where(lane < B_HEADS, -jnp.exp(a_log), 0.0)
    d_a = dt * (a_neg * LOG2_E)
    yield

    rr = lax.broadcasted_iota(jnp.int32, (rows, rows), 0)
    cc = lax.broadcasted_iota(jnp.int32, (rows, rows), 1)
    causal = cc <= rr
    tril = jnp.where(causal, 1.0, 0.0).astype(BF16)
    triu = jnp.where(rr <= cc, 1.0, 0.0).astype(BF16)
    d_a_parts = jnp.concatenate(_split_bf16(d_a, 3), axis=0)
    cum = _dot(jnp.concatenate([tril] * 3, axis=1), d_a_parts)
    cum_t = _dot_tn(d_a_parts, jnp.concatenate([triu] * 3, axis=0))
    yield

    def expand(x):
        return jnp.concatenate([jnp.broadcast_to(x[:, h:h + 1], (rows, B_P)) for h in range(B_HEADS)], axis=1)

    dt_full = expand(dt)
    cum_full = expand(cum)
    xdt = xs * dt_full
    lo = lane < B_P
    cb = [_dot_nt(cm[:, B_N * g:B_N * (g + 1)], bm[:, B_N * g:B_N * (g + 1)]) for g in range(B_G)]
    yield

    y_parts = []
    for p in range(B_HEADS // 2):
        sc = []
        for h in (2 * p, 2 * p + 1):
            diff = cum[:, h:h + 1] - cum_t[h:h + 1, :]
            decay = jnp.exp2(jnp.where(causal, diff, -jnp.inf))
            sc.append((cb[h // (B_HEADS // B_G)] * decay).astype(BF16))
        xp = xdt[:, LANE * p:LANE * (p + 1)]
        zero = jnp.zeros_like(xp)
        xx = jnp.concatenate([jnp.where(lo, xp, zero), jnp.where(lo, zero, xp)], axis=0).astype(BF16)
        y_parts.append(_dot(jnp.concatenate(sc, axis=1), xx))
        yield
    y = jnp.concatenate(y_parts, axis=1)

    st = st_ref[...]
    y = y + _dot(cm, st.astype(BF16)) * jnp.exp2(cum_full) + d_full * xs
    cum_last = cum_full[rows - 1:rows, :]
    upd = _dot_tn(bm, (xdt * jnp.exp2(cum_last - cum_full)).astype(BF16))
    sr = lax.broadcasted_iota(jnp.int32, (B_G * B_N, B_W), 0)
    sc = lax.broadcasted_iota(jnp.int32, (B_G * B_N, B_W), 1)
    st_ref[...] = st * jnp.exp2(cum_last) + jnp.where((sr // B_N) == (sc // (B_W // B_G)), upd, 0.0)
    z_raw = proj(O_Z, O_XBC)
    yield

    y = y * _silu(z_raw)
    gr = lax.broadcasted_iota(jnp.int32, (B_W, LANE), 0)
    gc = lax.broadcasted_iota(jnp.int32, (B_W, LANE), 1)
    group_ones = jnp.where((gr // (B_W // B_G)) == gc, 1.0, 0.0).astype(BF16)
    ss = _dot((y * y).astype(BF16), group_ones) * (1.0 / (B_W // B_G))
    ms = jnp.concatenate([jnp.broadcast_to(ss[:, g:g + 1], (rows, B_W // B_G)) for g in range(B_G)], axis=1)
    return y * lax.rsqrt(ms + RMS_EPS) * norm_w


def _s5(proj, b_bd, c_bd, pow_re, pow_im, d_skip, glu_w, glu_b, carry_ref):
    u = proj(O_U, O_DT)
    rows = u.shape[0]
    groups = rows // SUBLANE
    sub = lax.broadcasted_iota(jnp.int32, (SUBLANE, 1), 0)
    u3 = u.reshape(groups, SUBLANE, C_W)
    delayed = [u.astype(BF16)] + [jnp.where(sub >= d, pltpu.roll(u3, d, axis=1), 0.0).reshape(rows, C_W).astype(BF16)
                                  for d in range(1, S5_FOLD)]
    lo = lax.broadcasted_iota(jnp.int32, (1, LANE), 1) < QUAD_W
    quads = []
    for qd in range(C_G // 4):
        tile = slice(LANE * (qd // 2), LANE * (qd // 2 + 1))
        pieces = [x[:, tile] for x in delayed]
        if qd % 2 == 0:
            lhs = [jnp.where(lo, pieces[0], pltpu.roll(pieces[1], QUAD_W, axis=1)),
                   jnp.where(lo, pieces[2], pltpu.roll(pieces[3], QUAD_W, axis=1))]
        else:
            lhs = [jnp.where(lo, pltpu.roll(pieces[0], QUAD_W, axis=1), pieces[1]),
                   jnp.where(lo, pltpu.roll(pieces[2], QUAD_W, axis=1), pieces[3])]
        quads.append(_dot(jnp.concatenate(lhs, axis=1), b_bd[qd]))
    bu_re = jnp.concatenate([x[:, :4 * C_N] for x in quads], axis=1)
    bu_im = jnp.concatenate([x[:, 4 * C_N:] for x in quads], axis=1)
    yield
    x_re = bu_re.reshape(groups, SUBLANE, C_S)
    x_im = bu_im.reshape(groups, SUBLANE, C_S)
    d = S5_FOLD
    keep = sub >= d
    p_re = jnp.where(keep, pow_re[d - 1:d, :], 0.0)[None]
    p_im = jnp.where(keep, pow_im[d - 1:d, :], 0.0)[None]
    s_re = pltpu.roll(x_re, d, axis=1)
    s_im = pltpu.roll(x_im, d, axis=1)
    x_re, x_im = (x_re + (p_re * s_re - p_im * s_im), x_im + (p_re * s_im + p_im * s_re))
    yield

    c_re, c_im = carry_ref[0], carry_ref[1]
    g_re, g_im = [], []
    for i in range(groups):
        g_re.append(x_re[i] + (pow_re * c_re - pow_im * c_im))
        g_im.append(x_im[i] + (pow_re * c_im + pow_im * c_re))
        c_re, c_im = g_re[-1][SUBLANE - 1:, :], g_im[-1][SUBLANE - 1:, :]
        if i % 4 == 3:
            yield
    carry_ref[0] = c_re
    carry_ref[1] = c_im

    y = (_dot(jnp.concatenate(g_re, axis=0).astype(BF16), c_bd[:C_S, :])
         + _dot(jnp.concatenate(g_im, axis=0).astype(BF16), c_bd[C_S:, :]) + d_skip * u)
    yield
    y = jax.nn.gelu(y, approximate=True)
    gate = _dot(y.astype(BF16), glu_w) + glu_b
    yield
    return y * _sigmoid(gate)


def _delayed(gen, rounds):
    for _ in range(rounds):
        yield
    return (yield from gen)


def _interleave(gens):
    results = [None] * len(gens)
    live = list(range(len(gens)))
    while live:
        for i in list(live):
            try:
                next(gens[i])
            except StopIteration as stop:
                results[i] = stop.value
                live.remove(i)
    return results


def _mlp(prev, w1_ref, w2_ref, ln_g, ln_b, alpha):
    hb = prev.astype(BF16)
    ff = None
    for j in range(D_FF // MLP_CHUNK):
        cols = slice(j * MLP_CHUNK, (j + 1) * MLP_CHUNK)
        hid = jnp.maximum(_dot(hb, w1_ref[:, cols]), 0.0)
        yield
        part = _dot((hid * hid).astype(BF16), w2_ref[cols, :])
        ff = part if ff is None else ff + part
        yield
    return _layer_norm(alpha * prev + ff, ln_g, ln_b)


def _layer_kernel(alpha, front_pad, blocks_per_seq, first, h_ref, front_ref, w_in_ref, log_lb_ref, one_m_lb_ref, a_norm_ref,
                  conv_w_ref, conv_b_ref, dt_bias_ref, a_log_ref, d_full_ref, b_norm_ref,
                  s5_b_ref, s5_c_ref, s5_pre_ref, s5_pim_ref, s5_d_ref, glu_w_ref, glu_b_ref,
                  w_out_ref, ln1_g_ref, ln1_b_ref, w1_ref, w2_ref, ln2_g_ref, ln2_b_ref, o_ref,
                  a_state, b_hist, b_state, c_carry, h1_prev):
    step = pl.program_id(0)
    blk = step % blocks_per_seq
    rows = h_ref.shape[0]

    @pl.when(step == 0)
    def _():
        h1_prev[...] = jnp.zeros_like(h1_prev)

    @pl.when(blk == 0)
    def _():
        a_state[...] = jnp.zeros_like(a_state)
        b_hist[...] = jnp.zeros_like(b_hist)
        b_state[...] = jnp.zeros_like(b_state)
        c_carry[...] = jnp.zeros_like(c_carry)

    row = lax.broadcasted_iota(jnp.int32, (rows, 1), 0) + blk * rows
    pad_row = row < front_pad
    h = h_ref[...]
    if first:
        h = jnp.where(blk == 0, front_ref[...], h)
    hb = jnp.where(pad_row, 0.0, h).astype(BF16)

    groups = [(O_Q, O_I), (O_U, D_IN_PACKED), (O_Z, O_U), (O_I, O_Z)]
    projected = [(lo_col, hi_col, _dot(hb, w_in_ref[:, lo_col:hi_col])) for lo_col, hi_col in groups]

    def proj(lo_col, hi_col):
        for g_lo, g_hi, arr in projected:
            if g_lo <= lo_col and hi_col <= g_hi:
                return arr[:, lo_col - g_lo:hi_col - g_lo]
        raise ValueError((lo_col, hi_col))

    gens = [_delayed(_s5(proj, s5_b_ref[...], s5_c_ref[...], s5_pre_ref[...], s5_pim_ref[...],
                         s5_d_ref[...], glu_w_ref[...], glu_b_ref[...], c_carry), S5_START_ROUND),
            _delayed(_ssd(proj, conv_w_ref[...], conv_b_ref[...], dt_bias_ref[...], a_log_ref[...],
                          d_full_ref[...], b_norm_ref[...], pad_row, b_hist, b_state), SSD_START_ROUND)]
    tx = lax.broadcasted_iota(jnp.int32, (rows, rows), 0) ^ lax.broadcasted_iota(jnp.int32, (rows, rows), 1)
    group_masks = tuple(tx < c for c in HGRN_LEVELS[:0:-1])
    gens += [_hgrn2_pair(proj, p, log_lb_ref[...], one_m_lb_ref[...], a_norm_ref[...], pad_row, group_masks, a_state)
             for p in range(A_HEADS // 2)]
    gens.append(_mlp(h1_prev[...], w1_ref, w2_ref, ln2_g_ref[...], ln2_b_ref[...], alpha))
    y_c, y_b, *y_a, out_prev = _interleave(gens)
    o_ref[...] = out_prev
    y_all = jnp.concatenate([y.astype(BF16) for y in (*y_a, y_b, y_c)], axis=1)
    h1_prev[...] = _layer_norm(alpha * h + _dot(y_all, w_out_ref[...]), ln1_g_ref[...], ln1_b_ref[...])


def _layer_spec(arr, layer):
    zeros = (0,) * (arr.ndim - 1)
    return pl.BlockSpec((None,) + arr.shape[1:], lambda i: (layer,) + zeros, pipeline_mode=pl.Buffered(1))


def _layer_call(h_in, front, params, layer, alpha, front_pad, blocks_per_seq, n_seq, first, last):
    rows = ROW_BLOCK
    d = h_in.shape[1]
    n_blocks = n_seq * blocks_per_seq

    def unpadded(m):
        return (m // blocks_per_seq) * (blocks_per_seq - 1) + jnp.maximum(m % blocks_per_seq - 1, 0)

    def in_map(i):
        m = jnp.minimum(i, n_blocks - 1)
        return (unpadded(m) if first else m, 0)

    def out_map(i):
        m = jnp.maximum(i - 1, 0)
        return (unpadded(m) if last else m, 0)

    out_rows = (n_blocks - n_seq if last else n_blocks) * rows
    assert h_in.shape[0] == (n_blocks - n_seq if first else n_blocks) * rows
    return pl.pallas_call(
        functools.partial(_layer_kernel, alpha, front_pad, blocks_per_seq, first),
        grid=(n_blocks + 1,),
        in_specs=[pl.BlockSpec((rows, d), in_map), pl.BlockSpec((rows, d), lambda i: (0, 0))]
        + [_layer_spec(p, layer) for p in params],
        out_specs=pl.BlockSpec((rows, d), out_map),
        out_shape=jax.ShapeDtypeStruct((out_rows, d), F32),
        scratch_shapes=[
            pltpu.VMEM((A_HEADS // 2, LANE, LANE), F32),
            pltpu.VMEM((SUBLANE, XBC_W), F32),
            pltpu.VMEM((B_G * B_N, B_W), F32),
            pltpu.VMEM((2, 1, C_S), F32),
            pltpu.VMEM((rows, d), F32),
        ],
        compiler_params=pltpu.CompilerParams(
            dimension_semantics=("arbitrary",), vmem_limit_bytes=VMEM_LIMIT),
        name="layer",
    )(h_in, front, *params)


def _block_diag(blocks):
    g, r, c = blocks.shape
    tiled = jnp.tile(blocks.reshape(g * r, c), (1, g))
    rg = lax.broadcasted_iota(jnp.int32, (g * r, g * c), 0) // r
    cg = lax.broadcasted_iota(jnp.int32, (g * r, g * c), 1) // c
    return jnp.where(rg == cg, tiled, 0.0)


def _s5_params(a_re, a_im, log_dt, b_re, b_im, c_re, c_im):
    lam_re = jnp.minimum(a_re, S5_MAX_RE)
    lam_im = a_im
    dt = jnp.exp(log_dt)[:, None]
    mag = jnp.exp(lam_re * dt)
    lb_re = mag * jnp.cos(lam_im * dt)
    lb_im = mag * jnp.sin(lam_im * dt)
    den = jnp.square(lam_re) + jnp.square(lam_im)
    nr = lb_re - 1.0
    s_re = (nr * lam_re + lb_im * lam_im) / den
    s_im = (lb_im * lam_re - nr * lam_im) / den
    bb_re = s_re[..., None] * b_re - s_im[..., None] * b_im
    bb_im = s_re[..., None] * b_im + s_im[..., None] * b_re
    steps = jnp.arange(1, SUBLANE + 1, dtype=F32)[:, None, None]
    pmag = jnp.exp(lam_re * dt * steps)
    pw_re = pmag * jnp.cos(lam_im * dt * steps)
    pw_im = pmag * jnp.sin(lam_im * dt * steps)
    quads = C_G // 4
    b_rows = []
    for d in range(S5_FOLD):
        if d == 0:
            d_re, d_im = bb_re, bb_im
        else:
            pr, pi = pw_re[d - 1][..., None], pw_im[d - 1][..., None]
            d_re, d_im = bb_re * pr - bb_im * pi, bb_re * pi + bb_im * pr
        per_quad = lambda x: jax.vmap(_block_diag)(x.transpose(0, 2, 1).reshape(quads, 4, C_CH, C_N))
        b_rows.append(jnp.concatenate([per_quad(d_re), per_quad(d_im)], axis=2))
    b_bd = jnp.concatenate(b_rows, axis=1)
    c_bd = jnp.concatenate([_block_diag(c_re.transpose(0, 2, 1)), -_block_diag(c_im.transpose(0, 2, 1))], axis=0)
    return b_bd.astype(BF16), c_bd.astype(BF16), pw_re.reshape(SUBLANE, C_S), pw_im.reshape(SUBLANE, C_S)


def kernel(x, meta_tokens, w_in, hgrn_lb_logits, hgrn_norm_w, m2_conv_w, m2_conv_b, m2_dt_bias, m2_a_log, m2_d, m2_norm_w, s5_a_re, s5_a_im, s5_log_dt, s5_b_re, s5_b_im, s5_c_re, s5_c_im, s5_d, s5_glu_w, s5_glu_b, w_out, ln1_g, ln1_b, w_mlp_in, w_mlp_out, ln2_g, ln2_b):
    bsz, seq, d = x.shape
    depth = w_in.shape[0]
    alpha = (2 * depth) ** 0.25
    assert seq % ROW_BLOCK == 0
    front_pad = ROW_BLOCK - N_META
    front = jnp.concatenate([jnp.zeros((front_pad, d), x.dtype), meta_tokens.astype(x.dtype)], axis=0)

    f32 = lambda v: v.astype(F32)
    row3 = lambda v: f32(v).reshape(depth, 1, -1)
    lb_cum = jnp.cumsum(jax.nn.softmax(f32(hgrn_lb_logits), axis=0), axis=0)
    lower = lb_cum - lb_cum[0]
    pad_cols = D_IN_PACKED - O_DT - B_HEADS
    src_dt, src_u = O_U, O_U + B_HEADS
    w_in_p = jnp.concatenate([w_in[:, :, :src_dt], w_in[:, :, src_u:src_u + C_W], w_in[:, :, src_dt:src_u],
                              jnp.zeros((depth, d, pad_cols), w_in.dtype)], axis=2).astype(BF16)
    b_bd, c_bd, pow_re, pow_im = jax.vmap(_s5_params)(
        f32(s5_a_re), f32(s5_a_im), f32(s5_log_dt), f32(s5_b_re), f32(s5_b_im), f32(s5_c_re), f32(s5_c_im))
    pad6 = lambda v: jnp.concatenate([f32(v), jnp.zeros((depth, LANE - B_HEADS), F32)], axis=1).reshape(depth, 1, LANE)
    params = [
        w_in_p, row3(jnp.log(lower)), row3(1.0 - lower), row3(jnp.tile(hgrn_norm_w, (1, A_HEADS))),
        f32(m2_conv_w), row3(m2_conv_b), pad6(m2_dt_bias), pad6(m2_a_log),
        row3(jnp.repeat(m2_d, B_P, axis=1)), row3(m2_norm_w),
        b_bd, c_bd, pow_re, pow_im, row3(s5_d), s5_glu_w.astype(BF16), row3(s5_glu_b),
        w_out.astype(BF16), row3(ln1_g), row3(ln1_b),
        w_mlp_in.astype(BF16), w_mlp_out.astype(BF16), row3(ln2_g), row3(ln2_b),
    ]

    h2 = x.reshape(bsz * seq, d)
    for l in range(depth):
        h2 = _layer_call(h2, front, params, l, alpha, front_pad, seq // ROW_BLOCK + 1, bsz,
                         first=l == 0, last=l == depth - 1)
    return h2.reshape(bsz, seq, d)
```

```python
import functools

import jax
import jax.numpy as jnp
from jax import lax
from jax.experimental import pallas as pl
from jax.experimental.pallas import tpu as pltpu

F32 = jnp.float32
BF16 = jnp.bfloat16

D_MODEL = 1024
N_META = 16
A_HEADS, A_DK, A_W = 6, 64, 384
B_HEADS, B_P, B_W, B_G, B_N = 6, 64, 384, 2, 128
B_CONV = 4
C_G, C_CH, C_W, C_N = 16, 16, 256, 64
C_S = C_G * C_N
QUAD_W = 4 * C_CH
S5_FOLD = 4
D_FF = 4 * D_MODEL
LN_EPS = 1e-5
RMS_EPS = 1e-6
S5_MAX_RE = -1e-4
LOG2_E = 1.4426950408889634

O_Q, O_F, O_I, O_G, O_Z, O_XBC, O_U, O_DT = 0, 384, 768, 1152, 1536, 1920, 2816, 3072
D_IN_PACKED = 3200
XBC_W = 896

HGRN_RADIX = 4
HGRN_LEVELS = (1, 4, 16, 64)
ROW_BLOCK = 256
MLP_CHUNK = 512
START_ROUNDS = (3, 1, 1, 0, 0, 0)
S5_CARRY_PHASE = 8
LANE = 128
SUBLANE = 8
VMEM_LIMIT = 56 * 1024 * 1024


def _sigmoid(x):
    return 0.5 * jnp.tanh(0.5 * x) + 0.5


def _silu(x):
    hx = 0.5 * x
    return hx + hx * jnp.tanh(hx)


def _log1p_exp_neg_abs(x):
    return jnp.log(1.0 + jnp.exp(-jnp.abs(x)))


def _softplus(x):
    return jnp.maximum(x, 0.0) + _log1p_exp_neg_abs(x)


def _split_bf16(x, parts):
    out = []
    r = x
    for i in range(parts):
        p = r.astype(BF16)
        out.append(p)
        if i + 1 < parts:
            r = r - p.astype(F32)
    return out


def _dot(a, b):
    return jnp.dot(a, b, preferred_element_type=F32)


def _dot_nt(a, b):
    return lax.dot_general(a, b, (((1,), (1,)), ((), ())), preferred_element_type=F32)


def _dot_tn(a, b):
    return lax.dot_general(a, b, (((0,), (0,)), ((), ())), preferred_element_type=F32)


def _shift_rows(x, n):
    rows = x.shape[0]
    if n == 0:
        return x
    if n % SUBLANE == 0:
        z = jnp.zeros((abs(n), x.shape[1]), x.dtype)
        if n > 0:
            return jnp.concatenate([z, x[: rows - n]], axis=0)
        return jnp.concatenate([x[-n:], z], axis=0)
    return pltpu.roll(x, n % rows, axis=0)


def _layer_norm(x, g, b):
    mu = jnp.mean(x, axis=-1, keepdims=True)
    xc = x - mu
    var = jnp.mean(xc * xc, axis=-1, keepdims=True)
    return xc * lax.rsqrt(var + LN_EPS) * g + b


def _hgrn2_pair(proj, pair, log_lb, one_m_lb, norm_w, pad_row, group_masks, st_ref):
    sl = slice(LANE * pair, LANE * (pair + 1))
    col = lambda off: proj(off + LANE * pair, off + LANE * (pair + 1))
    f_raw = col(O_F)
    q_raw = col(O_Q)
    rows = f_raw.shape[0]
    assert rows == HGRN_RADIX * HGRN_LEVELS[-1] and HGRN_RADIX == 4
    yield
    q = _silu(q_raw)
    ls_pos = jnp.minimum(f_raw, 0.0) - _log1p_exp_neg_abs(f_raw)
    ls_neg = ls_pos - f_raw
    gap = f_raw - log_lb[:, sl]
    lf = ls_pos + jnp.maximum(-gap, 0.0) + _log1p_exp_neg_abs(gap)
    lf = jnp.where(pad_row, 0.0, lf)
    k = jnp.where(pad_row, 0.0, one_m_lb[:, sl] * jnp.exp(ls_neg))
    yield

    row = lax.broadcasted_iota(jnp.int32, (rows, 1), 0)
    lane = lax.broadcasted_iota(jnp.int32, (1, LANE), 1)
    lo = lane < A_DK
    zero_h = jnp.zeros((rows, LANE), BF16)

    def pack_heads(s1, s2, s3):
        s1r = pltpu.roll(s1, A_DK, axis=1)
        s2r = pltpu.roll(s2, A_DK, axis=1)
        even = jnp.concatenate([jnp.where(lo, s1, s2r), jnp.where(lo, s3, zero_h)], axis=1)
        odd = jnp.concatenate([jnp.where(lo, s1r, s2), jnp.where(lo, zero_h, s3)], axis=1)
        return even, odd

    lf = lf * LOG2_E
    cs, sf, tot = lf, jnp.zeros_like(lf), lf
    scores = {}
    for c in HGRN_LEVELS:
        j = (row // c) % HGRN_RADIX
        qt = (q * jnp.exp2(cs)).astype(BF16)
        up1, up2 = _shift_rows(tot, -c), _shift_rows(tot, -2 * c)
        e2 = sf + up1
        e3 = e2 + up2
        ke1, ke2, ke3 = [(k * jnp.exp2(e)).astype(BF16) for e in (sf, e2, e3)]
        qa, qb = pack_heads(*[jnp.where(j == i, qt, zero_h) for i in (1, 2, 3)])
        ka, kb = pack_heads(jnp.where(j == 0, ke1, zero_h),
                            jnp.where(j == 1, ke1, jnp.where(j == 0, ke2, zero_h)),
                            jnp.where(j == 2, ke1, jnp.where(j == 1, ke2, jnp.where(j == 0, ke3, zero_h))))
        scores[c] = (_dot_nt(qa, ka), _dot_nt(qb, kb))
        yield
        new_cs = cs + (jnp.where(j >= 1, _shift_rows(tot, c), 0.0) + jnp.where(j >= 2, _shift_rows(tot, 2 * c), 0.0)
                       + jnp.where(j >= 3, _shift_rows(tot, 3 * c), 0.0))
        new_sf = sf + (jnp.where(j <= 2, up1, 0.0) + jnp.where(j <= 1, up2, 0.0)
                       + jnp.where(j <= 0, _shift_rows(tot, -3 * c), 0.0))
        cs, sf = new_cs, new_sf
        tot = cs + sf
        yield

    q_in = (q * jnp.exp2(cs)).astype(BF16)
    k_out = (k * jnp.exp2(sf)).astype(BF16)
    decay_blk = jnp.exp2(tot[0:1, :])
    v_raw = col(O_I)
    vp = v_raw.astype(BF16)
    zero_b = jnp.zeros_like(vp)
    r2 = lax.broadcasted_iota(jnp.int32, (LANE, LANE), 0)
    c2 = lax.broadcasted_iota(jnp.int32, (LANE, LANE), 1)
    head_diag = (r2 // A_DK) == (c2 // A_DK)
    head_ones = jnp.where(head_diag, 1.0, 0.0).astype(BF16)
    o_diag = _dot((q * k).astype(BF16), head_ones) * vp.astype(F32)
    yield

    same64, same16, same4 = group_masks
    sc = []
    for hh in range(2):
        sc.append(jnp.where(same4, scores[1][hh],
                            jnp.where(same16, scores[4][hh],
                                      jnp.where(same64, scores[16][hh], scores[64][hh]))).astype(BF16))
        yield
    st = st_ref[pair]
    vv = jnp.concatenate([jnp.where(lo, vp, zero_b), jnp.where(lo, zero_b, vp)], axis=0)
    o = _dot(jnp.concatenate(sc, axis=1), vv) + _dot_nt(q_in, st.astype(BF16)) + o_diag
    upd = _dot_tn(vp, k_out)
    st_ref[pair] = st * decay_blk + jnp.where(head_diag, upd, 0.0)
    g_raw = col(O_G)
    yield

    ms = _dot((o * o).astype(BF16), head_ones) * (1.0 / A_DK)
    return o * lax.rsqrt(ms + RMS_EPS) * norm_w[:, sl] * _silu(g_raw)


def _ssd(proj, conv_w, conv_b, dt_bias, a_log, d_full, norm_w, pad_row, hist_ref, st_ref):
    xbc_raw = proj(O_XBC, O_U)
    dt_raw = proj(O_DT, D_IN_PACKED)
    rows = xbc_raw.shape[0]
    yield
    cat = jnp.concatenate([hist_ref[...], xbc_raw], axis=0)
    hist_ref[...] = xbc_raw[rows - SUBLANE:, :]
    acc = conv_b + conv_w[B_CONV - 1:B_CONV, :] * xbc_raw
    for n in (1, 2, 3):
        shifted = pltpu.roll(cat, n, axis=0)[SUBLANE:, :]
        acc = acc + conv_w[B_CONV - 1 - n:B_CONV - n, :] * shifted
    yield
    xbc = _silu(acc)
    xs = xbc[:, :B_W]
    bm = xbc[:, B_W:B_W + B_G * B_N].astype(BF16)
    cm = xbc[:, B_W + B_G * B_N:].astype(BF16)

    lane = lax.broadcasted_iota(jnp.int32, (1, LANE), 1)
    dt = jnp.where(pad_row, 0.0, _softplus(dt_raw + dt_bias))
    a_neg = jnp.where(lane < B_HEADS, -jnp.exp(a_log), 0.0)
    d_a = dt * (a_neg * LOG2_E)
    yield

    rr = lax.broadcasted_iota(jnp.int32, (rows, rows), 0)
    cc = lax.broadcasted_iota(jnp.int32, (rows, rows), 1)
    causal = cc <= rr
    tril = jnp.where(causal, 1.0, 0.0).astype(BF16)
    triu = jnp.where(rr <= cc, 1.0, 0.0).astype(BF16)
    d_a_parts = jnp.concatenate(_split_bf16(d_a, 3), axis=0)
    cum = _dot(jnp.concatenate([tril] * 3, axis=1), d_a_parts)
    cum_t = _dot_tn(d_a_parts, jnp.concatenate([triu] * 3, axis=0))
    yield

    def expand(x):
        return jnp.concatenate([jnp.broadcast_to(x[:, h:h + 1], (rows, B_P)) for h in range(B_HEADS)], axis=1)

    dt_full = expand(dt)
    cum_full = expand(cum)
    xdt = xs * dt_full
    lo = lane < B_P
    cb = [_dot_nt(cm[:, B_N * g:B_N * (g + 1)], bm[:, B_N * g:B_N * (g + 1)]) for g in range(B_G)]
    yield

    y_parts = []
    for p in range(B_HEADS // 2):
        sc = []
        for h in (2 * p, 2 * p + 1):
            diff = cum[:, h:h + 1] - cum_t[h:h + 1, :]
            decay = jnp.exp2(jnp.where(causal, diff, -jnp.inf))
            sc.append((cb[h // (B_HEADS // B_G)] * decay).astype(BF16))
        xp = xdt[:, LANE * p:LANE * (p + 1)]
        zero = jnp.zeros_like(xp)
        xx = jnp.concatenate([jnp.where(lo, xp, zero), jnp.where(lo, zero, xp)], axis=0).astype(BF16)
        y_parts.append(_dot(jnp.concatenate(sc, axis=1), xx))
        yield
    y = jnp.concatenate(y_parts, axis=1)

    st = st_ref[...]
    y = y + _dot(cm, st.astype(BF16)) * jnp.exp2(cum_full) + d_full * xs
    cum_last = cum_full[rows - 1:rows, :]
    upd = _dot_tn(bm, (xdt * jnp.exp2(cum_last - cum_full)).astype(BF16))
    sr = lax.broadcasted_iota(jnp.int32, (B_G * B_N, B_W), 0)
    sc = lax.broadcasted_iota(jnp.int32, (B_G * B_N, B_W), 1)
    st_ref[...] = st * jnp.exp2(cum_last) + jnp.where((sr // B_N) == (sc // (B_W // B_G)), upd, 0.0)
    z_raw = proj(O_Z, O_XBC)
    yield

    y = y * _silu(z_raw)
    gr = lax.broadcasted_iota(jnp.int32, (B_W, LANE), 0)
    gc = lax.broadcasted_iota(jnp.int32, (B_W, LANE), 1)
    group_ones = jnp.where((gr // (B_W // B_G)) == gc, 1.0, 0.0).astype(BF16)
    ss = _dot((y * y).astype(BF16), group_ones) * (1.0 / (B_W // B_G))
    ms = jnp.concatenate([jnp.broadcast_to(ss[:, g:g + 1], (rows, B_W // B_G)) for g in range(B_G)], axis=1)
    return y * lax.rsqrt(ms + RMS_EPS) * norm_w


def _s5(proj, b_bd, c_bd, pow_re, pow_im, d_skip, glu_w, glu_b, carry_ref):
    u = proj(O_U, O_DT)
    rows = u.shape[0]
    groups = rows // SUBLANE
    sub = lax.broadcasted_iota(jnp.int32, (SUBLANE, 1), 0)
    u3 = u.reshape(groups, SUBLANE, C_W)
    delayed = [u.astype(BF16)] + [jnp.where(sub >= d, pltpu.roll(u3, d, axis=1), 0.0).reshape(rows, C_W).astype(BF16)
                                  for d in range(1, S5_FOLD)]
    lo = lax.broadcasted_iota(jnp.int32, (1, LANE), 1) < QUAD_W
    quads = []
    for qd in range(C_G // 4):
        tile = slice(LANE * (qd // 2), LANE * (qd // 2 + 1))
        pieces = [x[:, tile] for x in delayed]
        if qd % 2 == 0:
            lhs = [jnp.where(lo, pieces[0], pltpu.roll(pieces[1], QUAD_W, axis=1)),
                   jnp.where(lo, pieces[2], pltpu.roll(pieces[3], QUAD_W, axis=1))]
        else:
            lhs = [jnp.where(lo, pltpu.roll(pieces[0], QUAD_W, axis=1), pieces[1]),
                   jnp.where(lo, pltpu.roll(pieces[2], QUAD_W, axis=1), pieces[3])]
        quads.append(_dot(jnp.concatenate(lhs, axis=1), b_bd[qd]))
    bu_re = jnp.concatenate([x[:, :4 * C_N] for x in quads], axis=1)
    bu_im = jnp.concatenate([x[:, 4 * C_N:] for x in quads], axis=1)
    yield
    x_re = bu_re.reshape(groups, SUBLANE, C_S)
    x_im = bu_im.reshape(groups, SUBLANE, C_S)
    d = S5_FOLD
    keep = sub >= d
    p_re = jnp.where(keep, pow_re[d - 1:d, :], 0.0)[None]
    p_im = jnp.where(keep, pow_im[d - 1:d, :], 0.0)[None]
    s_re = pltpu.roll(x_re, d, axis=1)
    s_im = pltpu.roll(x_im, d, axis=1)
    x_re, x_im = (x_re + (p_re * s_re - p_im * s_im), x_im + (p_re * s_im + p_im * s_re))
    yield

    c_re, c_im = carry_ref[0], carry_ref[1]
    g_re, g_im = [], []
    for i in range(groups):
        g_re.append(x_re[i] + (pow_re * c_re - pow_im * c_im))
        g_im.append(x_im[i] + (pow_re * c_im + pow_im * c_re))
        c_re, c_im = g_re[-1][SUBLANE - 1:, :], g_im[-1][SUBLANE - 1:, :]
        if i % S5_CARRY_PHASE == S5_CARRY_PHASE - 1:
            yield
    carry_ref[0] = c_re
    carry_ref[1] = c_im

    y = (_dot(jnp.concatenate(g_re, axis=0).astype(BF16), c_bd[:C_S, :])
         + _dot(jnp.concatenate(g_im, axis=0).astype(BF16), c_bd[C_S:, :]) + d_skip * u)
    yield
    y = jax.nn.gelu(y, approximate=True)
    gate = _dot(y.astype(BF16), glu_w) + glu_b
    yield
    return y * _sigmoid(gate)


def _delayed(gen, rounds):
    for _ in range(rounds):
        yield
    return (yield from gen)


def _interleave(gens):
    results = [None] * len(gens)
    live = list(range(len(gens)))
    while live:
        for i in list(live):
            try:
                next(gens[i])
            except StopIteration as stop:
                results[i] = stop.value
                live.remove(i)
    return results


def _mlp(prev, w1_ref, w2_ref, ln_g, ln_b, alpha):
    hb = prev.astype(BF16)
    ff = None
    for j in range(D_FF // MLP_CHUNK):
        cols = slice(j * MLP_CHUNK, (j + 1) * MLP_CHUNK)
        hid = jnp.maximum(_dot(hb, w1_ref[:, cols]), 0.0)
        yield
        part = _dot((hid * hid).astype(BF16), w2_ref[cols, :])
        ff = part if ff is None else ff + part
        yield
    return _layer_norm(alpha * prev + ff, ln_g, ln_b)


def _layer_kernel(alpha, front_pad, blocks_per_seq, first, h_ref, front_ref, w_in_ref, log_lb_ref, one_m_lb_ref, a_norm_ref,
                  conv_w_ref, conv_b_ref, dt_bias_ref, a_log_ref, d_full_ref, b_norm_ref,
                  s5_b_ref, s5_c_ref, s5_pre_ref, s5_pim_ref, s5_d_ref, glu_w_ref, glu_b_ref,
                  w_out_ref, ln1_g_ref, ln1_b_ref, w1_ref, w2_ref, ln2_g_ref, ln2_b_ref, o_ref,
                  a_state, b_hist, b_state, c_carry, h1_prev):
    step = pl.program_id(0)
    blk = step % blocks_per_seq
    rows = h_ref.shape[0]

    @pl.when(step == 0)
    def _():
        h1_prev[...] = jnp.zeros_like(h1_prev)

    @pl.when(blk == 0)
    def _():
        a_state[...] = jnp.zeros_like(a_state)
        b_hist[...] = jnp.zeros_like(b_hist)
        b_state[...] = jnp.zeros_like(b_state)
        c_carry[...] = jnp.zeros_like(c_carry)

    row = lax.broadcasted_iota(jnp.int32, (rows, 1), 0) + blk * rows
    pad_row = row < front_pad
    h = h_ref[...]
    if first:
        h = jnp.where(blk == 0, front_ref[...], h)
    hb = jnp.where(pad_row, 0.0, h).astype(BF16)

    groups = [(O_Q, O_I), (O_U, D_IN_PACKED), (O_Z, O_U), (O_I, O_Z)]
    projected = [(lo_col, hi_col, _dot(hb, w_in_ref[:, lo_col:hi_col])) for lo_col, hi_col in groups]

    def proj(lo_col, hi_col):
        for g_lo, g_hi, arr in projected:
            if g_lo <= lo_col and hi_col <= g_hi:
                return arr[:, lo_col - g_lo:hi_col - g_lo]
        raise ValueError((lo_col, hi_col))

    gens = [_delayed(_s5(proj, s5_b_ref[...], s5_c_ref[...], s5_pre_ref[...], s5_pim_ref[...],
                         s5_d_ref[...], glu_w_ref[...], glu_b_ref[...], c_carry), START_ROUNDS[0]),
            _delayed(_ssd(proj, conv_w_ref[...], conv_b_ref[...], dt_bias_ref[...], a_log_ref[...],
                          d_full_ref[...], b_norm_ref[...], pad_row, b_hist, b_state), START_ROUNDS[1])]
    tx = lax.broadcasted_iota(jnp.int32, (rows, rows), 0) ^ lax.broadcasted_iota(jnp.int32, (rows, rows), 1)
    group_masks = tuple(tx < c for c in HGRN_LEVELS[:0:-1])
    gens += [_delayed(_hgrn2_pair(proj, p, log_lb_ref[...], one_m_lb_ref[...], a_norm_ref[...], pad_row, group_masks,
                                  a_state), START_ROUNDS[2 + p])
             for p in range(A_HEADS // 2)]
    gens.append(_delayed(_mlp(h1_prev[...], w1_ref, w2_ref, ln2_g_ref[...], ln2_b_ref[...], alpha), START_ROUNDS[5]))
    y_c, y_b, *y_a, out_prev = _interleave(gens)
    o_ref[...] = out_prev
    y_all = jnp.concatenate([y.astype(BF16) for y in (*y_a, y_b, y_c)], axis=1)
    h1_prev[...] = _layer_norm(alpha * h + _dot(y_all, w_out_ref[...]), ln1_g_ref[...], ln1_b_ref[...])


def _layer_spec(arr, layer):
    zeros = (0,) * (arr.ndim - 1)
    return pl.BlockSpec((None,) + arr.shape[1:], lambda i: (layer,) + zeros, pipeline_mode=pl.Buffered(1))


def _layer_call(h_in, front, params, layer, alpha, front_pad, blocks_per_seq, n_seq, first, last):
    rows = ROW_BLOCK
    d = h_in.shape[1]
    n_blocks = n_seq * blocks_per_seq

    def unpadded(m):
        return (m // blocks_per_seq) * (blocks_per_seq - 1) + jnp.maximum(m % blocks_per_seq - 1, 0)

    def in_map(i):
        m = jnp.minimum(i, n_blocks - 1)
        return (unpadded(m) if first else m, 0)

    def out_map(i):
        m = jnp.maximum(i - 1, 0)
        return (unpadded(m) if last else m, 0)

    out_rows = (n_blocks - n_seq if last else n_blocks) * rows
    assert h_in.shape[0] == (n_blocks - n_seq if first else n_blocks) * rows
    return pl.pallas_call(
        functools.partial(_layer_kernel, alpha, front_pad, blocks_per_seq, first),
        grid=(n_blocks + 1,),
        in_specs=[pl.BlockSpec((rows, d), in_map), pl.BlockSpec((rows, d), lambda i: (0, 0))]
        + [_layer_spec(p, layer) for p in params],
        out_specs=pl.BlockSpec((rows, d), out_map),
        out_shape=jax.ShapeDtypeStruct((out_rows, d), F32),
        scratch_shapes=[
            pltpu.VMEM((A_HEADS // 2, LANE, LANE), F32),
            pltpu.VMEM((SUBLANE, XBC_W), F32),
            pltpu.VMEM((B_G * B_N, B_W), F32),
            pltpu.VMEM((2, 1, C_S), F32),
            pltpu.VMEM((rows, d), F32),
        ],
        compiler_params=pltpu.CompilerParams(
            dimension_semantics=("arbitrary",), vmem_limit_bytes=VMEM_LIMIT),
        name="layer",
    )(h_in, front, *params)


def _block_diag(blocks):
    g, r, c = blocks.shape
    tiled = jnp.tile(blocks.reshape(g * r, c), (1, g))
    rg = lax.broadcasted_iota(jnp.int32, (g * r, g * c), 0) // r
    cg = lax.broadcasted_iota(jnp.int32, (g * r, g * c), 1) // c
    return jnp.where(rg == cg, tiled, 0.0)


def _s5_params(a_re, a_im, log_dt, b_re, b_im, c_re, c_im):
    lam_re = jnp.minimum(a_re, S5_MAX_RE)
    lam_im = a_im
    dt = jnp.exp(log_dt)[:, None]
    mag = jnp.exp(lam_re * dt)
    lb_re = mag * jnp.cos(lam_im * dt)
    lb_im = mag * jnp.sin(lam_im * dt)
    den = jnp.square(lam_re) + jnp.square(lam_im)
    nr = lb_re - 1.0
    s_re = (nr * lam_re + lb_im * lam_im) / den
    s_im = (lb_im * lam_re - nr * lam_im) / den
    bb_re = s_re[..., None] * b_re - s_im[..., None] * b_im
    bb_im = s_re[..., None] * b_im + s_im[..., None] * b_re
    steps = jnp.arange(1, SUBLANE + 1, dtype=F32)[:, None, None]
    pmag = jnp.exp(lam_re * dt * steps)
    pw_re = pmag * jnp.cos(lam_im * dt * steps)
    pw_im = pmag * jnp.sin(lam_im * dt * steps)
    quads = C_G // 4
    b_rows = []
    for d in range(S5_FOLD):
        if d == 0:
            d_re, d_im = bb_re, bb_im
        else:
            pr, pi = pw_re[d - 1][..., None], pw_im[d - 1][..., None]
            d_re, d_im = bb_re * pr - bb_im * pi, bb_re * pi + bb_im * pr
        per_quad = lambda x: jax.vmap(_block_diag)(x.transpose(0, 2, 1).reshape(quads, 4, C_CH, C_N))
        b_rows.append(jnp.concatenate([per_quad(d_re), per_quad(d_im)], axis=2))
    b_bd = jnp.concatenate(b_rows, axis=1)
    c_bd = jnp.concatenate([_block_diag(c_re.transpose(0, 2, 1)), -_block_diag(c_im.transpose(0, 2, 1))], axis=0)
    return b_bd.astype(BF16), c_bd.astype(BF16), pw_re.reshape(SUBLANE, C_S), pw_im.reshape(SUBLANE, C_S)


def kernel(x, meta_tokens, w_in, hgrn_lb_logits, hgrn_norm_w, m2_conv_w, m2_conv_b, m2_dt_bias, m2_a_log, m2_d, m2_norm_w, s5_a_re, s5_a_im, s5_log_dt, s5_b_re, s5_b_im, s5_c_re, s5_c_im, s5_d, s5_glu_w, s5_glu_b, w_out, ln1_g, ln1_b, w_mlp_in, w_mlp_out, ln2_g, ln2_b):
    bsz, seq, d = x.shape
    depth = w_in.shape[0]
    alpha = (2 * depth) ** 0.25
    assert seq % ROW_BLOCK == 0
    front_pad = ROW_BLOCK - N_META
    front = jnp.concatenate([jnp.zeros((front_pad, d), x.dtype), meta_tokens.astype(x.dtype)], axis=0)

    f32 = lambda v: v.astype(F32)
    row3 = lambda v: f32(v).reshape(depth, 1, -1)
    lb_cum = jnp.cumsum(jax.nn.softmax(f32(hgrn_lb_logits), axis=0), axis=0)
    lower = lb_cum - lb_cum[0]
    pad_cols = D_IN_PACKED - O_DT - B_HEADS
    src_dt, src_u = O_U, O_U + B_HEADS
    w_in_p = jnp.concatenate([w_in[:, :, :src_dt], w_in[:, :, src_u:src_u + C_W], w_in[:, :, src_dt:src_u],
                              jnp.zeros((depth, d, pad_cols), w_in.dtype)], axis=2).astype(BF16)
    b_bd, c_bd, pow_re, pow_im = jax.vmap(_s5_params)(
        f32(s5_a_re), f32(s5_a_im), f32(s5_log_dt), f32(s5_b_re), f32(s5_b_im), f32(s5_c_re), f32(s5_c_im))
    pad6 = lambda v: jnp.concatenate([f32(v), jnp.zeros((depth, LANE - B_HEADS), F32)], axis=1).reshape(depth, 1, LANE)
    params = [
        w_in_p, row3(jnp.log(lower)), row3(1.0 - lower), row3(jnp.tile(hgrn_norm_w, (1, A_HEADS))),
        f32(m2_conv_w), row3(m2_conv_b), pad6(m2_dt_bias), pad6(m2_a_log),
        row3(jnp.repeat(m2_d, B_P, axis=1)), row3(m2_norm_w),
        b_bd, c_bd, pow_re, pow_im, row3(s5_d), s5_glu_w.astype(BF16), row3(s5_glu_b),
        w_out.astype(BF16), row3(ln1_g), row3(ln1_b),
        w_mlp_in.astype(BF16), w_mlp_out.astype(BF16), row3(ln2_g), row3(ln2_b),
    ]

    h2 = x.reshape(bsz * seq, d)
    for l in range(depth):
        h2 = _layer_call(h2, front, params, l, alpha, front_pad, seq // ROW_BLOCK + 1, bsz,
                         first=l == 0, last=l == depth - 1)
    return h2.reshape(bsz, seq, d)
```

```python
import functools

import jax
import jax.numpy as jnp
from jax import lax
from jax.experimental import pallas as pl
from jax.experimental.pallas import tpu as pltpu

F32 = jnp.float32
BF16 = jnp.bfloat16

D_MODEL = 1024
N_META = 16
A_HEADS, A_DK, A_W = 6, 64, 384
B_HEADS, B_P, B_W, B_G, B_N = 6, 64, 384, 2, 128
B_CONV = 4
C_G, C_CH, C_W, C_N = 16, 16, 256, 64
C_S = C_G * C_N
QUAD_W = 4 * C_CH
S5_FOLD = 4
D_FF = 4 * D_MODEL
LN_EPS = 1e-5
RMS_EPS = 1e-6
S5_MAX_RE = -1e-4
LOG2_E = 1.4426950408889634

O_Q, O_F, O_I, O_G, O_Z, O_XBC, O_U, O_DT = 0, 384, 768, 1152, 1536, 1920, 2816, 3072
D_IN_PACKED = 3200
XBC_W = 896

HGRN_RADIX = 4
HGRN_LEVELS = (1, 4, 16, 64)
ROW_BLOCK = 256
MLP_CHUNK = 512
START_ROUNDS = (3, 1, 1, 0, 0, 0)
S5_CARRY_PHASE = 8
LANE = 128
SUBLANE = 8
VMEM_LIMIT = 56 * 1024 * 1024


def _sigmoid(x):
    return 0.5 * jnp.tanh(0.5 * x) + 0.5


def _silu(x):
    hx = 0.5 * x
    return hx + hx * jnp.tanh(hx)


def _log1p_exp_neg_abs(x):
    return jnp.log(1.0 + jnp.exp(-jnp.abs(x)))


def _softplus(x):
    return jnp.maximum(x, 0.0) + _log1p_exp_neg_abs(x)


def _split_bf16(x, parts):
    out = []
    r = x
    for i in range(parts):
        p = r.astype(BF16)
        out.append(p)
        if i + 1 < parts:
            r = r - p.astype(F32)
    return out


def _dot(a, b):
    return jnp.dot(a, b, preferred_element_type=F32)


def _dot_nt(a, b):
    return lax.dot_general(a, b, (((1,), (1,)), ((), ())), preferred_element_type=F32)


def _dot_tn(a, b):
    return lax.dot_general(a, b, (((0,), (0,)), ((), ())), preferred_element_type=F32)


def _shift_rows(x, n):
    rows = x.shape[0]
    if n == 0:
        return x
    if n % SUBLANE == 0:
        z = jnp.zeros((abs(n), x.shape[1]), x.dtype)
        if n > 0:
            return jnp.concatenate([z, x[: rows - n]], axis=0)
        return jnp.concatenate([x[-n:], z], axis=0)
    return pltpu.roll(x, n % rows, axis=0)


def _layer_norm(x, g, b):
    mu = jnp.mean(x, axis=-1, keepdims=True)
    xc = x - mu
    var = jnp.mean(xc * xc, axis=-1, keepdims=True)
    return xc * lax.rsqrt(var + LN_EPS) * g + b


def _hgrn2_pair(proj, pair, log_lb, one_m_lb, norm_w, pad_row, group_masks, st_ref):
    sl = slice(LANE * pair, LANE * (pair + 1))
    col = lambda off: proj(off + LANE * pair, off + LANE * (pair + 1))
    f_raw = col(O_F)
    q_raw = col(O_Q)
    rows = f_raw.shape[0]
    assert rows == HGRN_RADIX * HGRN_LEVELS[-1] and HGRN_RADIX == 4
    yield
    q = _silu(q_raw)
    ls_pos = jnp.minimum(f_raw, 0.0) - _log1p_exp_neg_abs(f_raw)
    ls_neg = ls_pos - f_raw
    gap = f_raw - log_lb[:, sl]
    lf = ls_pos + jnp.maximum(-gap, 0.0) + _log1p_exp_neg_abs(gap)
    lf = jnp.where(pad_row, 0.0, lf)
    k = jnp.where(pad_row, 0.0, one_m_lb[:, sl] * jnp.exp(ls_neg))
    yield

    row = lax.broadcasted_iota(jnp.int32, (rows, 1), 0)
    lane = lax.broadcasted_iota(jnp.int32, (1, LANE), 1)
    lo = lane < A_DK
    zero_h = jnp.zeros((rows, LANE), BF16)

    def pack_heads(s1, s2, s3):
        s1r = pltpu.roll(s1, A_DK, axis=1)
        s2r = pltpu.roll(s2, A_DK, axis=1)
        even = jnp.concatenate([jnp.where(lo, s1, s2r), jnp.where(lo, s3, zero_h)], axis=1)
        odd = jnp.concatenate([jnp.where(lo, s1r, s2), jnp.where(lo, zero_h, s3)], axis=1)
        return even, odd

    lf = lf * LOG2_E
    cs, sf, tot = lf, jnp.zeros_like(lf), lf
    scores = {}
    for c in HGRN_LEVELS:
        j = (row // c) % HGRN_RADIX
        qt = (q * jnp.exp2(cs)).astype(BF16)
        up1, up2 = _shift_rows(tot, -c), _shift_rows(tot, -2 * c)
        e2 = sf + up1
        e3 = e2 + up2
        ke1, ke2, ke3 = [(k * jnp.exp2(e)).astype(BF16) for e in (sf, e2, e3)]
        qa, qb = pack_heads(*[jnp.where(j == i, qt, zero_h) for i in (1, 2, 3)])
        ka, kb = pack_heads(jnp.where(j == 0, ke1, zero_h),
                            jnp.where(j == 1, ke1, jnp.where(j == 0, ke2, zero_h)),
                            jnp.where(j == 2, ke1, jnp.where(j == 1, ke2, jnp.where(j == 0, ke3, zero_h))))
        scores[c] = (_dot_nt(qa, ka), _dot_nt(qb, kb))
        yield
        new_cs = cs + (jnp.where(j >= 1, _shift_rows(tot, c), 0.0) + jnp.where(j >= 2, _shift_rows(tot, 2 * c), 0.0)
                       + jnp.where(j >= 3, _shift_rows(tot, 3 * c), 0.0))
        new_sf = sf + (jnp.where(j <= 2, up1, 0.0) + jnp.where(j <= 1, up2, 0.0)
                       + jnp.where(j <= 0, _shift_rows(tot, -3 * c), 0.0))
        cs, sf = new_cs, new_sf
        tot = cs + sf
        yield

    q_in = (q * jnp.exp2(cs)).astype(BF16)
    k_out = (k * jnp.exp2(sf)).astype(BF16)
    decay_blk = jnp.exp2(tot[0:1, :])
    v_raw = col(O_I)
    vp = v_raw.astype(BF16)
    zero_b = jnp.zeros_like(vp)
    r2 = lax.broadcasted_iota(jnp.int32, (LANE, LANE), 0)
    c2 = lax.broadcasted_iota(jnp.int32, (LANE, LANE), 1)
    head_diag = (r2 // A_DK) == (c2 // A_DK)
    head_ones = jnp.where(head_diag, 1.0, 0.0).astype(BF16)
    o_diag = _dot((q * k).astype(BF16), head_ones) * vp.astype(F32)
    yield

    same64, same16, same4 = group_masks
    sc = []
    for hh in range(2):
        sc.append(jnp.where(same4, scores[1][hh],
                            jnp.where(same16, scores[4][hh],
                                      jnp.where(same64, scores[16][hh], scores[64][hh]))).astype(BF16))
        yield
    st = st_ref[pair]
    vv = jnp.concatenate([jnp.where(lo, vp, zero_b), jnp.where(lo, zero_b, vp)], axis=0)
    o = _dot(jnp.concatenate(sc, axis=1), vv) + _dot_nt(q_in, st.astype(BF16)) + o_diag
    upd = _dot_tn(vp, k_out)
    st_ref[pair] = st * decay_blk + jnp.where(head_diag, upd, 0.0)
    g_raw = col(O_G)
    yield

    ms = _dot((o * o).astype(BF16), head_ones) * (1.0 / A_DK)
    return o * lax.rsqrt(ms + RMS_EPS) * norm_w[:, sl] * _silu(g_raw)


def _ssd(proj, conv_w, conv_b, dt_bias, a_log, d_full, norm_w, pad_row, hist_ref, st_ref):
    xbc_raw = proj(O_XBC, O_U)
    dt_raw = proj(O_DT, D_IN_PACKED)
    rows = xbc_raw.shape[0]
    yield
    cat = jnp.concatenate([hist_ref[...], xbc_raw], axis=0)
    hist_ref[...] = xbc_raw[rows - SUBLANE:, :]
    acc = conv_b + conv_w[B_CONV - 1:B_CONV, :] * xbc_raw
    for n in (1, 2, 3):
        shifted = pltpu.roll(cat, n, axis=0)[SUBLANE:, :]
        acc = acc + conv_w[B_CONV - 1 - n:B_CONV - n, :] * shifted
    yield
    xbc = _silu(acc)
    xs = xbc[:, :B_W]
    bm = xbc[:, B_W:B_W + B_G * B_N].astype(BF16)
    cm = xbc[:, B_W + B_G * B_N:].astype(BF16)

    lane = lax.broadcasted_iota(jnp.int32, (1, LANE), 1)
    dt = jnp.where(pad_row, 0.0, _softplus(dt_raw + dt_bias))
    a_neg = jnp.where(lane < B_HEADS, -jnp.exp(a_log), 0.0)
    d_a = dt * (a_neg * LOG2_E)
    yield

    rr = lax.broadcasted_iota(jnp.int32, (rows, rows), 0)
    cc = lax.broadcasted_iota(jnp.int32, (rows, rows), 1)
    causal = cc <= rr
    tril = jnp.where(causal, 1.0, 0.0).astype(BF16)
    triu = jnp.where(rr <= cc, 1.0, 0.0).astype(BF16)
    d_a_parts = jnp.concatenate(_split_bf16(d_a, 3), axis=0)
    cum = _dot(jnp.concatenate([tril] * 3, axis=1), d_a_parts)
    cum_t = _dot_tn(d_a_parts, jnp.concatenate([triu] * 3, axis=0))
    yield

    def expand(x):
        return jnp.concatenate([jnp.broadcast_to(x[:, h:h + 1], (rows, B_P)) for h in range(B_HEADS)], axis=1)

    dt_full = expand(dt)
    cum_full = expand(cum)
    xdt = xs * dt_full
    lo = lane < B_P
    cb = [_dot_nt(cm[:, B_N * g:B_N * (g + 1)], bm[:, B_N * g:B_N * (g + 1)]) for g in range(B_G)]
    yield

    y_parts = []
    for p in range(B_HEADS // 2):
        sc = []
        for h in (2 * p, 2 * p + 1):
            diff = cum[:, h:h + 1] - cum_t[h:h + 1, :]
            decay = jnp.exp2(jnp.where(causal, diff, -jnp.inf))
            sc.append((cb[h // (B_HEADS // B_G)] * decay).astype(BF16))
        xp = xdt[:, LANE * p:LANE * (p + 1)]
        zero = jnp.zeros_like(xp)
        xx = jnp.concatenate([jnp.where(lo, xp, zero), jnp.where(lo, zero, xp)], axis=0).astype(BF16)
        y_parts.append(_dot(jnp.concatenate(sc, axis=1), xx))
        yield
    y = jnp.concatenate(y_parts, axis=1)

    st = st_ref[...]
    y = y + _dot(cm, st.astype(BF16)) * jnp.exp2(cum_full) + d_full * xs
    cum_last = cum_full[rows - 1:rows, :]
    upd = _dot_tn(bm, (xdt * jnp.exp2(cum_last - cum_full)).astype(BF16))
    sr = lax.broadcasted_iota(jnp.int32, (B_G * B_N, B_W), 0)
    sc = lax.broadcasted_iota(jnp.int32, (B_G * B_N, B_W), 1)
    st_ref[...] = st * jnp.exp2(cum_last) + jnp.where((sr // B_N) == (sc // (B_W // B_G)), upd, 0.0)
    z_raw = proj(O_Z, O_XBC)
    yield

    y = y * _silu(z_raw)
    gr = lax.broadcasted_iota(jnp.int32, (B_W, LANE), 0)
    gc = lax.broadcasted_iota(jnp.int32, (B_W, LANE), 1)
    group_ones = jnp.where((gr // (B_W // B_G)) == gc, 1.0, 0.0).astype(BF16)
    ss = _dot((y * y).astype(BF16), group_ones) * (1.0 / (B_W // B_G))
    ms = jnp.concatenate([jnp.broadcast_to(ss[:, g:g + 1], (rows, B_W // B_G)) for g in range(B_G)], axis=1)
    return y * lax.rsqrt(ms + RMS_EPS) * norm_w


def _s5(proj, b_bd, c_bd, pow_re, pow_im, d_skip, glu_w, glu_b, carry_ref):
    u = proj(O_U, O_DT)
    rows = u.shape[0]
    groups = rows // SUBLANE
    sub = lax.broadcasted_iota(jnp.int32, (SUBLANE, 1), 0)
    u3 = u.reshape(groups, SUBLANE, C_W)
    delayed = [u.astype(BF16)] + [jnp.where(sub >= d, pltpu.roll(u3, d, axis=1), 0.0).reshape(rows, C_W).astype(BF16)
                                  for d in range(1, S5_FOLD)]
    lo = lax.broadcasted_iota(jnp.int32, (1, LANE), 1) < QUAD_W
    quads = []
    for qd in range(C_G // 4):
        tile = slice(LANE * (qd // 2), LANE * (qd // 2 + 1))
        pieces = [x[:, tile] for x in delayed]
        if qd % 2 == 0:
            lhs = [jnp.where(lo, pieces[0], pltpu.roll(pieces[1], QUAD_W, axis=1)),
                   jnp.where(lo, pieces[2], pltpu.roll(pieces[3], QUAD_W, axis=1))]
        else:
            lhs = [jnp.where(lo, pltpu.roll(pieces[0], QUAD_W, axis=1), pieces[1]),
                   jnp.where(lo, pltpu.roll(pieces[2], QUAD_W, axis=1), pieces[3])]
        quads.append(_dot(jnp.concatenate(lhs, axis=1), b_bd[qd]))
    bu_re = jnp.concatenate([x[:, :4 * C_N] for x in quads], axis=1)
    bu_im = jnp.concatenate([x[:, 4 * C_N:] for x in quads], axis=1)
    yield
    x_re = bu_re.reshape(groups, SUBLANE, C_S)
    x_im = bu_im.reshape(groups, SUBLANE, C_S)
    d = S5_FOLD
    keep = sub >= d
    p_re = jnp.where(keep, pow_re[d - 1:d, :], 0.0)[None]
    p_im = jnp.where(keep, pow_im[d - 1:d, :], 0.0)[None]
    s_re = pltpu.roll(x_re, d, axis=1)
    s_im = pltpu.roll(x_im, d, axis=1)
    x_re, x_im = (x_re + (p_re * s_re - p_im * s_im), x_im + (p_re * s_im + p_im * s_re))
    yield

    c_re, c_im = carry_ref[0], carry_ref[1]
    g_re, g_im = [], []
    for i in range(groups):
        g_re.append(x_re[i] + (pow_re * c_re - pow_im * c_im))
        g_im.append(x_im[i] + (pow_re * c_im + pow_im * c_re))
        c_re, c_im = g_re[-1][SUBLANE - 1:, :], g_im[-1][SUBLANE - 1:, :]
        if i % S5_CARRY_PHASE == S5_CARRY_PHASE - 1:
            yield
    carry_ref[0] = c_re
    carry_ref[1] = c_im

    y = (_dot(jnp.concatenate(g_re, axis=0).astype(BF16), c_bd[:C_S, :])
         + _dot(jnp.concatenate(g_im, axis=0).astype(BF16), c_bd[C_S:, :]) + d_skip * u)
    yield
    y = jax.nn.gelu(y, approximate=True)
    gate = _dot(y.astype(BF16), glu_w) + glu_b
    yield
    return y * _sigmoid(gate)


def _delayed(gen, rounds):
    for _ in range(rounds):
        yield
    return (yield from gen)


def _interleave(gens):
    results = [None] * len(gens)
    live = list(range(len(gens)))
    while live:
        for i in list(live):
            try:
                next(gens[i])
            except StopIteration as stop:
                results[i] = stop.value
                live.remove(i)
    return results


def _mlp(prev, w1_ref, w2_ref, ln_g, ln_b, alpha):
    hb = prev.astype(BF16)
    ff = None
    for j in range(D_FF // MLP_CHUNK):
        cols = slice(j * MLP_CHUNK, (j + 1) * MLP_CHUNK)
        hid = jnp.maximum(_dot(hb, w1_ref[:, cols]), 0.0)
        yield
        part = _dot((hid * hid).astype(BF16), w2_ref[cols, :])
        ff = part if ff is None else ff + part
        yield
    return _layer_norm(alpha * prev + ff, ln_g, ln_b)


def _layer_kernel(alpha, front_pad, blocks_per_seq, first, h_ref, front_ref, w_in_ref, w_tail_ref, log_lb_ref, one_m_lb_ref,
                  a_norm_ref,
                  conv_w_ref, conv_b_ref, dt_bias_ref, a_log_ref, d_full_ref, b_norm_ref,
                  s5_b_ref, s5_c_ref, s5_pre_ref, s5_pim_ref, s5_d_ref, glu_w_ref, glu_b_ref,
                  w_out_ref, ln1_g_ref, ln1_b_ref, w1_ref, w2_ref, ln2_g_ref, ln2_b_ref, o_ref,
                  a_state, b_hist, b_state, c_carry, h1_prev):
    step = pl.program_id(0)
    blk = step % blocks_per_seq
    rows = h_ref.shape[0]

    @pl.when(step == 0)
    def _():
        h1_prev[...] = jnp.zeros_like(h1_prev)

    @pl.when(blk == 0)
    def _():
        a_state[...] = jnp.zeros_like(a_state)
        b_hist[...] = jnp.zeros_like(b_hist)
        b_state[...] = jnp.zeros_like(b_state)
        c_carry[...] = jnp.zeros_like(c_carry)

    row = lax.broadcasted_iota(jnp.int32, (rows, 1), 0) + blk * rows
    pad_row = row < front_pad
    h = h_ref[...]
    if first:
        h = jnp.where(blk == 0, front_ref[...], h)
    hb = jnp.where(pad_row, 0.0, h).astype(BF16)

    groups = [(O_Q, O_I), (O_U, D_IN_PACKED), (O_Z, O_U), (O_I, O_Z)]
    projected = [(lo_col, hi_col, _dot(hb, w_tail_ref[...] if lo_col == O_U else w_in_ref[:, lo_col:hi_col]))
                 for lo_col, hi_col in groups]

    def proj(lo_col, hi_col):
        for g_lo, g_hi, arr in projected:
            if g_lo <= lo_col and hi_col <= g_hi:
                return arr[:, lo_col - g_lo:hi_col - g_lo]
        raise ValueError((lo_col, hi_col))

    gens = [_delayed(_s5(proj, s5_b_ref[...], s5_c_ref[...], s5_pre_ref[...], s5_pim_ref[...],
                         s5_d_ref[...], glu_w_ref[...], glu_b_ref[...], c_carry), START_ROUNDS[0]),
            _delayed(_ssd(proj, conv_w_ref[...], conv_b_ref[...], dt_bias_ref[...], a_log_ref[...],
                          d_full_ref[...], b_norm_ref[...], pad_row, b_hist, b_state), START_ROUNDS[1])]
    tx = lax.broadcasted_iota(jnp.int32, (rows, rows), 0) ^ lax.broadcasted_iota(jnp.int32, (rows, rows), 1)
    group_masks = tuple(tx < c for c in HGRN_LEVELS[:0:-1])
    gens += [_delayed(_hgrn2_pair(proj, p, log_lb_ref[...], one_m_lb_ref[...], a_norm_ref[...], pad_row, group_masks,
                                  a_state), START_ROUNDS[2 + p])
             for p in range(A_HEADS // 2)]
    gens.append(_delayed(_mlp(h1_prev[...], w1_ref, w2_ref, ln2_g_ref[...], ln2_b_ref[...], alpha), START_ROUNDS[5]))
    y_c, y_b, *y_a, out_prev = _interleave(gens)
    o_ref[...] = out_prev
    y_all = jnp.concatenate([y.astype(BF16) for y in (*y_a, y_b, y_c)], axis=1)
    h1_prev[...] = _layer_norm(alpha * h + _dot(y_all, w_out_ref[...]), ln1_g_ref[...], ln1_b_ref[...])


def _layer_spec(arr, layer):
    zeros = (0,) * (arr.ndim - 1)
    return pl.BlockSpec((None,) + arr.shape[1:], lambda i: (layer,) + zeros, pipeline_mode=pl.Buffered(1))


def _layer_call(h_in, front, params, layer, alpha, front_pad, blocks_per_seq, n_seq, first, last):
    rows = ROW_BLOCK
    d = h_in.shape[1]
    n_blocks = n_seq * blocks_per_seq

    def unpadded(m):
        return (m // blocks_per_seq) * (blocks_per_seq - 1) + jnp.maximum(m % blocks_per_seq - 1, 0)

    def in_map(i):
        m = jnp.minimum(i, n_blocks - 1)
        return (unpadded(m) if first else m, 0)

    def out_map(i):
        m = jnp.maximum(i - 1, 0)
        return (unpadded(m) if last else m, 0)

    out_rows = (n_blocks - n_seq if last else n_blocks) * rows
    assert h_in.shape[0] == (n_blocks - n_seq if first else n_blocks) * rows
    return pl.pallas_call(
        functools.partial(_layer_kernel, alpha, front_pad, blocks_per_seq, first),
        grid=(n_blocks + 1,),
        in_specs=[pl.BlockSpec((rows, d), in_map), pl.BlockSpec((rows, d), lambda i: (0, 0)),
                  pl.BlockSpec((None, d, O_U), lambda i: (layer, 0, 0), pipeline_mode=pl.Buffered(1))]
        + [_layer_spec(p, layer) for p in params[1:]],
        out_specs=pl.BlockSpec((rows, d), out_map),
        out_shape=jax.ShapeDtypeStruct((out_rows, d), F32),
        scratch_shapes=[
            pltpu.VMEM((A_HEADS // 2, LANE, LANE), F32),
            pltpu.VMEM((SUBLANE, XBC_W), F32),
            pltpu.VMEM((B_G * B_N, B_W), F32),
            pltpu.VMEM((2, 1, C_S), F32),
            pltpu.VMEM((rows, d), F32),
        ],
        compiler_params=pltpu.CompilerParams(
            dimension_semantics=("arbitrary",), vmem_limit_bytes=VMEM_LIMIT),
        name="layer",
    )(h_in, front, *params)


def _block_diag(blocks):
    g, r, c = blocks.shape
    tiled = jnp.tile(blocks.reshape(g * r, c), (1, g))
    rg = lax.broadcasted_iota(jnp.int32, (g * r, g * c), 0) // r
    cg = lax.broadcasted_iota(jnp.int32, (g * r, g * c), 1) // c
    return jnp.where(rg == cg, tiled, 0.0)


def _s5_params(a_re, a_im, log_dt, b_re, b_im, c_re, c_im):
    lam_re = jnp.minimum(a_re, S5_MAX_RE)
    lam_im = a_im
    dt = jnp.exp(log_dt)[:, None]
    mag = jnp.exp(lam_re * dt)
    lb_re = mag * jnp.cos(lam_im * dt)
    lb_im = mag * jnp.sin(lam_im * dt)
    den = jnp.square(lam_re) + jnp.square(lam_im)
    nr = lb_re - 1.0
    s_re = (nr * lam_re + lb_im * lam_im) / den
    s_im = (lb_im * lam_re - nr * lam_im) / den
    bb_re = s_re[..., None] * b_re - s_im[..., None] * b_im
    bb_im = s_re[..., None] * b_im + s_im[..., None] * b_re
    steps = jnp.arange(1, SUBLANE + 1, dtype=F32)[:, None, None]
    pmag = jnp.exp(lam_re * dt * steps)
    pw_re = pmag * jnp.cos(lam_im * dt * steps)
    pw_im = pmag * jnp.sin(lam_im * dt * steps)
    quads = C_G // 4
    b_rows = []
    for d in range(S5_FOLD):
        if d == 0:
            d_re, d_im = bb_re, bb_im
        else:
            pr, pi = pw_re[d - 1][..., None], pw_im[d - 1][..., None]
            d_re, d_im = bb_re * pr - bb_im * pi, bb_re * pi + bb_im * pr
        per_quad = lambda x: jax.vmap(_block_diag)(x.transpose(0, 2, 1).reshape(quads, 4, C_CH, C_N))
        b_rows.append(jnp.concatenate([per_quad(d_re), per_quad(d_im)], axis=2))
    b_bd = jnp.concatenate(b_rows, axis=1)
    c_bd = jnp.concatenate([_block_diag(c_re.transpose(0, 2, 1)), -_block_diag(c_im.transpose(0, 2, 1))], axis=0)
    return b_bd.astype(BF16), c_bd.astype(BF16), pw_re.reshape(SUBLANE, C_S), pw_im.reshape(SUBLANE, C_S)


def kernel(x, meta_tokens, w_in, hgrn_lb_logits, hgrn_norm_w, m2_conv_w, m2_conv_b, m2_dt_bias, m2_a_log, m2_d, m2_norm_w, s5_a_re, s5_a_im, s5_log_dt, s5_b_re, s5_b_im, s5_c_re, s5_c_im, s5_d, s5_glu_w, s5_glu_b, w_out, ln1_g, ln1_b, w_mlp_in, w_mlp_out, ln2_g, ln2_b):
    bsz, seq, d = x.shape
    depth = w_in.shape[0]
    alpha = (2 * depth) ** 0.25
    assert seq % ROW_BLOCK == 0
    front_pad = ROW_BLOCK - N_META
    front = jnp.concatenate([jnp.zeros((front_pad, d), x.dtype), meta_tokens.astype(x.dtype)], axis=0)

    f32 = lambda v: v.astype(F32)
    row3 = lambda v: f32(v).reshape(depth, 1, -1)
    lb_cum = jnp.cumsum(jax.nn.softmax(f32(hgrn_lb_logits), axis=0), axis=0)
    lower = lb_cum - lb_cum[0]
    pad_cols = D_IN_PACKED - O_DT - B_HEADS
    src_dt, src_u = O_U, O_U + B_HEADS
    w_tail = jnp.concatenate([w_in[:, :, src_u:src_u + C_W], w_in[:, :, src_dt:src_u],
                              jnp.zeros((depth, d, pad_cols), w_in.dtype)], axis=2).astype(BF16)
    b_bd, c_bd, pow_re, pow_im = jax.vmap(_s5_params)(
        f32(s5_a_re), f32(s5_a_im), f32(s5_log_dt), f32(s5_b_re), f32(s5_b_im), f32(s5_c_re), f32(s5_c_im))
    pad6 = lambda v: jnp.concatenate([f32(v), jnp.zeros((depth, LANE - B_HEADS), F32)], axis=1).reshape(depth, 1, LANE)
    params = [
        w_in.astype(BF16), w_tail, row3(jnp.log(lower)), row3(1.0 - lower), row3(jnp.tile(hgrn_norm_w, (1, A_HEADS))),
        f32(m2_conv_w), row3(m2_conv_b), pad6(m2_dt_bias), pad6(m2_a_log),
        row3(jnp.repeat(m2_d, B_P, axis=1)), row3(m2_norm_w),
        b_bd, c_bd, pow_re, pow_im, row3(s5_d), s5_glu_w.astype(BF16), row3(s5_glu_b),
        w_out.astype(BF16), row3(ln1_g), row3(ln1_b),
        w_mlp_in.astype(BF16), w_mlp_out.astype(BF16), row3(ln2_g), row3(ln2_b),
    ]

    h2 = x.reshape(bsz * seq, d)
    for l in range(depth):
        h2 = _layer_call(h2, front, params, l, alpha, front_pad, seq // ROW_BLOCK + 1, bsz,
                         first=l == 0, last=l == depth - 1)
    return h2.reshape(bsz, seq, d)
```

```python
import functools

import jax
import jax.numpy as jnp
from jax import lax
from jax.experimental import pallas as pl
from jax.experimental.pallas import tpu as pltpu

F32 = jnp.float32
BF16 = jnp.bfloat16

D_MODEL = 1024
N_META = 16
A_HEADS, A_DK, A_W = 6, 64, 384
B_HEADS, B_P, B_W, B_G, B_N = 6, 64, 384, 2, 128
B_CONV = 4
C_G, C_CH, C_W, C_N = 16, 16, 256, 64
C_S = C_G * C_N
QUAD_W = 4 * C_CH
S5_FOLD = 4
D_FF = 4 * D_MODEL
LN_EPS = 1e-5
RMS_EPS = 1e-6
S5_MAX_RE = -1e-4
LOG2_E = 1.4426950408889634

O_Q, O_F, O_I, O_G, O_Z, O_XBC, O_U, O_DT = 0, 384, 768, 1152, 1536, 1920, 2816, 3072
D_IN_PACKED = 3200
XBC_W = 896

HGRN_RADIX = 4
HGRN_LEVELS = (1, 4, 16, 64)
ROW_BLOCK = 256
MLP_CHUNK = 512
START_ROUNDS = (3, 1, 1, 0, 0, 0)
S5_CARRY_PHASE = 8
LANE = 128
SUBLANE = 8
VMEM_LIMIT = 56 * 1024 * 1024


def _sigmoid(x):
    return 0.5 * jnp.tanh(0.5 * x) + 0.5


def _silu(x):
    hx = 0.5 * x
    return hx + hx * jnp.tanh(hx)


def _log1p_exp_neg_abs(x):
    return jnp.log(1.0 + jnp.exp(-jnp.abs(x)))


def _softplus(x):
    return jnp.maximum(x, 0.0) + _log1p_exp_neg_abs(x)


def _split_bf16(x, parts):
    out = []
    r = x
    for i in range(parts):
        p = r.astype(BF16)
        out.append(p)
        if i + 1 < parts:
            r = r - p.astype(F32)
    return out


def _dot(a, b):
    return jnp.dot(a, b, preferred_element_type=F32)


def _dot_nt(a, b):
    return lax.dot_general(a, b, (((1,), (1,)), ((), ())), preferred_element_type=F32)


def _dot_tn(a, b):
    return lax.dot_general(a, b, (((0,), (0,)), ((), ())), preferred_element_type=F32)


def _shift_rows(x, n):
    rows = x.shape[0]
    if n == 0:
        return x
    if n % SUBLANE == 0:
        z = jnp.zeros((abs(n), x.shape[1]), x.dtype)
        if n > 0:
            return jnp.concatenate([z, x[: rows - n]], axis=0)
        return jnp.concatenate([x[-n:], z], axis=0)
    return pltpu.roll(x, n % rows, axis=0)


def _layer_norm(x, g, b):
    mu = jnp.mean(x, axis=-1, keepdims=True)
    xc = x - mu
    var = jnp.mean(xc * xc, axis=-1, keepdims=True)
    return xc * lax.rsqrt(var + LN_EPS) * g + b


def _hgrn2_pair(proj, pair, log_lb, one_m_lb, norm_w, pad_row, group_masks, st_ref):
    sl = slice(LANE * pair, LANE * (pair + 1))
    col = lambda off: proj(off + LANE * pair, off + LANE * (pair + 1))
    f_raw = col(O_F)
    q_raw = col(O_Q)
    rows = f_raw.shape[0]
    assert rows == HGRN_RADIX * HGRN_LEVELS[-1] and HGRN_RADIX == 4
    yield
    q = _silu(q_raw)
    ls_pos = jnp.minimum(f_raw, 0.0) - _log1p_exp_neg_abs(f_raw)
    ls_neg = ls_pos - f_raw
    gap = f_raw - log_lb[:, sl]
    lf = ls_pos + jnp.maximum(-gap, 0.0) + _log1p_exp_neg_abs(gap)
    lf = jnp.where(pad_row, 0.0, lf)
    k = jnp.where(pad_row, 0.0, one_m_lb[:, sl] * jnp.exp(ls_neg))
    yield

    row = lax.broadcasted_iota(jnp.int32, (rows, 1), 0)
    lane = lax.broadcasted_iota(jnp.int32, (1, LANE), 1)
    lo = lane < A_DK
    zero_h = jnp.zeros((rows, LANE), BF16)

    def pack_heads(s1, s2, s3):
        s1r = pltpu.roll(s1, A_DK, axis=1)
        s2r = pltpu.roll(s2, A_DK, axis=1)
        even = jnp.concatenate([jnp.where(lo, s1, s2r), jnp.where(lo, s3, zero_h)], axis=1)
        odd = jnp.concatenate([jnp.where(lo, s1r, s2), jnp.where(lo, zero_h, s3)], axis=1)
        return even, odd

    lf = lf * LOG2_E
    cs, sf, tot = lf, jnp.zeros_like(lf), lf
    scores = {}
    for c in HGRN_LEVELS:
        j = (row // c) % HGRN_RADIX
        qt = (q * jnp.exp2(cs)).astype(BF16)
        up1, up2 = _shift_rows(tot, -c), _shift_rows(tot, -2 * c)
        e2 = sf + up1
        e3 = e2 + up2
        ke1, ke2, ke3 = [(k * jnp.exp2(e)).astype(BF16) for e in (sf, e2, e3)]
        qa, qb = pack_heads(*[jnp.where(j == i, qt, zero_h) for i in (1, 2, 3)])
        ka, kb = pack_heads(jnp.where(j == 0, ke1, zero_h),
                            jnp.where(j == 1, ke1, jnp.where(j == 0, ke2, zero_h)),
                            jnp.where(j == 2, ke1, jnp.where(j == 1, ke2, jnp.where(j == 0, ke3, zero_h))))
        scores[c] = (_dot_nt(qa, ka), _dot_nt(qb, kb))
        yield
        new_cs = cs + (jnp.where(j >= 1, _shift_rows(tot, c), 0.0) + jnp.where(j >= 2, _shift_rows(tot, 2 * c), 0.0)
                       + jnp.where(j >= 3, _shift_rows(tot, 3 * c), 0.0))
        new_sf = sf + (jnp.where(j <= 2, up1, 0.0) + jnp.where(j <= 1, up2, 0.0)
                       + jnp.where(j <= 0, _shift_rows(tot, -3 * c), 0.0))
        cs, sf = new_cs, new_sf
        tot = cs + sf
        yield

    q_in = (q * jnp.exp2(cs)).astype(BF16)
    k_out = (k * jnp.exp2(sf)).astype(BF16)
    decay_blk = jnp.exp2(tot[0:1, :])
    v_raw = col(O_I)
    vp = v_raw.astype(BF16)
    zero_b = jnp.zeros_like(vp)
    r2 = lax.broadcasted_iota(jnp.int32, (LANE, LANE), 0)
    c2 = lax.broadcasted_iota(jnp.int32, (LANE, LANE), 1)
    head_diag = (r2 // A_DK) == (c2 // A_DK)
    head_ones = jnp.where(head_diag, 1.0, 0.0).astype(BF16)
    o_diag = _dot((q * k).astype(BF16), head_ones) * vp.astype(F32)
    yield

    same64, same16, same4 = group_masks
    sc = []
    for hh in range(2):
        sc.append(jnp.where(same4, scores[1][hh],
                            jnp.where(same16, scores[4][hh],
                                      jnp.where(same64, scores[16][hh], scores[64][hh]))).astype(BF16))
        yield
    st = st_ref[pair]
    vv = jnp.concatenate([jnp.where(lo, vp, zero_b), jnp.where(lo, zero_b, vp)], axis=0)
    o = _dot(jnp.concatenate(sc, axis=1), vv) + _dot_nt(q_in, st.astype(BF16)) + o_diag
    upd = _dot_tn(vp, k_out)
    st_ref[pair] = st * decay_blk + jnp.where(head_diag, upd, 0.0)
    g_raw = col(O_G)
    yield

    ms = _dot((o * o).astype(BF16), head_ones) * (1.0 / A_DK)
    return o * lax.rsqrt(ms + RMS_EPS) * norm_w[:, sl] * _silu(g_raw)


def _ssd(proj, conv_w, conv_b, dt_bias, a_log, d_full, norm_w, pad_row, hist_ref, st_ref):
    xbc_raw = proj(O_XBC, O_U)
    dt_raw = proj(O_DT, D_IN_PACKED)
    rows = xbc_raw.shape[0]
    yield
    cat = jnp.concatenate([hist_ref[...], xbc_raw], axis=0)
    hist_ref[...] = xbc_raw[rows - SUBLANE:, :]
    acc = conv_b + conv_w[B_CONV - 1:B_CONV, :] * xbc_raw
    for n in (1, 2, 3):
        shifted = pltpu.roll(cat, n, axis=0)[SUBLANE:, :]
        acc = acc + conv_w[B_CONV - 1 - n:B_CONV - n, :] * shifted
    yield
    xbc = _silu(acc)
    xs = xbc[:, :B_W]
    bm = xbc[:, B_W:B_W + B_G * B_N].astype(BF16)
    cm = xbc[:, B_W + B_G * B_N:].astype(BF16)

    lane = lax.broadcasted_iota(jnp.int32, (1, LANE), 1)
    dt = jnp.where(pad_row, 0.0, _softplus(dt_raw + dt_bias))
    a_neg = jnp.where(lane < B_HEADS, -jnp.exp(a_log), 0.0)
    d_a = dt * (a_neg * LOG2_E)
    yield

    rr = lax.broadcasted_iota(jnp.int32, (rows, rows), 0)
    cc = lax.broadcasted_iota(jnp.int32, (rows, rows), 1)
    causal = cc <= rr
    tril = jnp.where(causal, 1.0, 0.0).astype(BF16)
    triu = jnp.where(rr <= cc, 1.0, 0.0).astype(BF16)
    d_a_parts = jnp.concatenate(_split_bf16(d_a, 3), axis=0)
    cum = _dot(jnp.concatenate([tril] * 3, axis=1), d_a_parts)
    cum_t = _dot_tn(d_a_parts, jnp.concatenate([triu] * 3, axis=0))
    yield

    def expand(x):
        return jnp.concatenate([jnp.broadcast_to(x[:, h:h + 1], (rows, B_P)) for h in range(B_HEADS)], axis=1)

    dt_full = expand(dt)
    cum_full = expand(cum)
    xdt = xs * dt_full
    lo = lane < B_P
    cb = [_dot_nt(cm[:, B_N * g:B_N * (g + 1)], bm[:, B_N * g:B_N * (g + 1)]) for g in range(B_G)]
    yield

    y_parts = []
    for p in range(B_HEADS // 2):
        sc = []
        for h in (2 * p, 2 * p + 1):
            diff = cum[:, h:h + 1] - cum_t[h:h + 1, :]
            decay = jnp.exp2(jnp.where(causal, diff, -jnp.inf))
            sc.append((cb[h // (B_HEADS // B_G)] * decay).astype(BF16))
        xp = xdt[:, LANE * p:LANE * (p + 1)]
        zero = jnp.zeros_like(xp)
        xx = jnp.concatenate([jnp.where(lo, xp, zero), jnp.where(lo, zero, xp)], axis=0).astype(BF16)
        y_parts.append(_dot(jnp.concatenate(sc, axis=1), xx))
        yield
    y = jnp.concatenate(y_parts, axis=1)

    st = st_ref[...]
    y = y + _dot(cm, st.astype(BF16)) * jnp.exp2(cum_full) + d_full * xs
    cum_last = cum_full[rows - 1:rows, :]
    upd = _dot_tn(bm, (xdt * jnp.exp2(cum_last - cum_full)).astype(BF16))
    sr = lax.broadcasted_iota(jnp.int32, (B_G * B_N, B_W), 0)
    sc = lax.broadcasted_iota(jnp.int32, (B_G * B_N, B_W), 1)
    st_ref[...] = st * jnp.exp2(cum_last) + jnp.where((sr // B_N) == (sc // (B_W // B_G)), upd, 0.0)
    z_raw = proj(O_Z, O_XBC)
    yield

    y = y * _silu(z_raw)
    gr = lax.broadcasted_iota(jnp.int32, (B_W, LANE), 0)
    gc = lax.broadcasted_iota(jnp.int32, (B_W, LANE), 1)
    group_ones = jnp.where((gr // (B_W // B_G)) == gc, 1.0, 0.0).astype(BF16)
    ss = _dot((y * y).astype(BF16), group_ones) * (1.0 / (B_W // B_G))
    ms = jnp.concatenate([jnp.broadcast_to(ss[:, g:g + 1], (rows, B_W // B_G)) for g in range(B_G)], axis=1)
    return y * lax.rsqrt(ms + RMS_EPS) * norm_w


def _s5(proj, b_bd, c_bd, pow_re, pow_im, d_skip, glu_w, glu_b, carry_ref):
    u = proj(O_U, O_DT)
    rows = u.shape[0]
    groups = rows // SUBLANE
    sub = lax.broadcasted_iota(jnp.int32, (SUBLANE, 1), 0)
    u3 = u.reshape(groups, SUBLANE, C_W)
    delayed = [u.astype(BF16)] + [jnp.where(sub >= d, pltpu.roll(u3, d, axis=1), 0.0).reshape(rows, C_W).astype(BF16)
                                  for d in range(1, S5_FOLD)]
    lo = lax.broadcasted_iota(jnp.int32, (1, LANE), 1) < QUAD_W
    quads = []
    for qd in range(C_G // 4):
        tile = slice(LANE * (qd // 2), LANE * (qd // 2 + 1))
        pieces = [x[:, tile] for x in delayed]
        if qd % 2 == 0:
            lhs = [jnp.where(lo, pieces[0], pltpu.roll(pieces[1], QUAD_W, axis=1)),
                   jnp.where(lo, pieces[2], pltpu.roll(pieces[3], QUAD_W, axis=1))]
        else:
            lhs = [jnp.where(lo, pltpu.roll(pieces[0], QUAD_W, axis=1), pieces[1]),
                   jnp.where(lo, pltpu.roll(pieces[2], QUAD_W, axis=1), pieces[3])]
        quads.append(_dot(jnp.concatenate(lhs, axis=1), b_bd[qd]))
    bu_re = jnp.concatenate([x[:, :4 * C_N] for x in quads], axis=1)
    bu_im = jnp.concatenate([x[:, 4 * C_N:] for x in quads], axis=1)
    yield
    x_re = bu_re.reshape(groups, SUBLANE, C_S)
    x_im = bu_im.reshape(groups, SUBLANE, C_S)
    d = S5_FOLD
    keep = sub >= d
    p_re = jnp.where(keep, pow_re[d - 1:d, :], 0.0)[None]
    p_im = jnp.where(keep, pow_im[d - 1:d, :], 0.0)[None]
    s_re = pltpu.roll(x_re, d, axis=1)
    s_im = pltpu.roll(x_im, d, axis=1)
    x_re, x_im = (x_re + (p_re * s_re - p_im * s_im), x_im + (p_re * s_im + p_im * s_re))
    yield

    c_re, c_im = carry_ref[0], carry_ref[1]
    g_re, g_im = [], []
    for i in range(groups):
        g_re.append(x_re[i] + (pow_re * c_re - pow_im * c_im))
        g_im.append(x_im[i] + (pow_re * c_im + pow_im * c_re))
        c_re, c_im = g_re[-1][SUBLANE - 1:, :], g_im[-1][SUBLANE - 1:, :]
        if i % S5_CARRY_PHASE == S5_CARRY_PHASE - 1:
            yield
    carry_ref[0] = c_re
    carry_ref[1] = c_im

    y = (_dot(jnp.concatenate(g_re, axis=0).astype(BF16), c_bd[:C_S, :])
         + _dot(jnp.concatenate(g_im, axis=0).astype(BF16), c_bd[C_S:, :]) + d_skip * u)
    yield
    y = jax.nn.gelu(y, approximate=True)
    gate = _dot(y.astype(BF16), glu_w) + glu_b
    yield
    return y * _sigmoid(gate)


def _delayed(gen, rounds):
    for _ in range(rounds):
        yield
    return (yield from gen)


def _interleave(gens):
    results = [None] * len(gens)
    live = list(range(len(gens)))
    while live:
        for i in list(live):
            try:
                next(gens[i])
            except StopIteration as stop:
                results[i] = stop.value
                live.remove(i)
    return results


def _mlp(prev, w1_ref, w2_ref, ln_g, ln_b, alpha):
    hb = prev.astype(BF16)
    ff = None
    for j in range(D_FF // MLP_CHUNK):
        cols = slice(j * MLP_CHUNK, (j + 1) * MLP_CHUNK)
        hid = jnp.maximum(_dot(hb, w1_ref[:, cols]), 0.0)
        yield
        part = _dot((hid * hid).astype(BF16), w2_ref[cols, :])
        ff = part if ff is None else ff + part
        yield
    return _layer_norm(alpha * prev + ff, ln_g, ln_b)


def _layer_kernel(alpha, front_pad, blocks_per_seq, first, h_ref, front_ref, w_in_ref, w_tail_ref, log_lb_ref, one_m_lb_ref,
                  a_norm_ref,
                  conv_w_ref, conv_b_ref, dt_bias_ref, a_log_ref, d_full_ref, b_norm_ref,
                  s5_b_ref, s5_c_ref, s5_pre_ref, s5_pim_ref, s5_d_ref, glu_w_ref, glu_b_ref,
                  w_out_ref, ln1_g_ref, ln1_b_ref, w1_ref, w2_ref, ln2_g_ref, ln2_b_ref, o_ref,
                  a_state, b_hist, b_state, c_carry, h1_prev):
    step = pl.program_id(0)
    blk = step % blocks_per_seq
    rows = h_ref.shape[0]

    @pl.when(step == 0)
    def _():
        h1_prev[...] = jnp.zeros_like(h1_prev)

    @pl.when(blk == 0)
    def _():
        a_state[...] = jnp.zeros_like(a_state)
        b_hist[...] = jnp.zeros_like(b_hist)
        b_state[...] = jnp.zeros_like(b_state)
        c_carry[...] = jnp.zeros_like(c_carry)

    row = lax.broadcasted_iota(jnp.int32, (rows, 1), 0) + blk * rows
    pad_row = row < front_pad
    h = h_ref[...]
    if first:
        h = jnp.where(blk == 0, front_ref[...], h)
    hb = jnp.where(pad_row, 0.0, h).astype(BF16)

    groups = [(O_Q, O_I), (O_U, D_IN_PACKED), (O_Z, O_U), (O_I, O_Z)]
    projected = [(lo_col, hi_col, _dot(hb, w_tail_ref[...] if lo_col == O_U else w_in_ref[:, lo_col:hi_col]))
                 for lo_col, hi_col in groups]

    def proj(lo_col, hi_col):
        for g_lo, g_hi, arr in projected:
            if g_lo <= lo_col and hi_col <= g_hi:
                return arr[:, lo_col - g_lo:hi_col - g_lo]
        raise ValueError((lo_col, hi_col))

    gens = [_delayed(_s5(proj, s5_b_ref[...], s5_c_ref[...], s5_pre_ref[...], s5_pim_ref[...],
                         s5_d_ref[...], glu_w_ref[...], glu_b_ref[...], c_carry), START_ROUNDS[0]),
            _delayed(_ssd(proj, conv_w_ref[...], conv_b_ref[...], dt_bias_ref[...], a_log_ref[...],
                          d_full_ref[...], b_norm_ref[...], pad_row, b_hist, b_state), START_ROUNDS[1])]
    tx = lax.broadcasted_iota(jnp.int32, (rows, rows), 0) ^ lax.broadcasted_iota(jnp.int32, (rows, rows), 1)
    group_masks = tuple(tx < c for c in HGRN_LEVELS[:0:-1])
    gens += [_delayed(_hgrn2_pair(proj, p, log_lb_ref[...], one_m_lb_ref[...], a_norm_ref[...], pad_row, group_masks,
                                  a_state), START_ROUNDS[2 + p])
             for p in range(A_HEADS // 2)]
    prev = h1_prev[...]
    prev_b = prev.astype(BF16)

    def mlp_chunk(j, ff):
        c0 = pl.multiple_of(j * MLP_CHUNK, MLP_CHUNK)
        hid = jnp.maximum(_dot(prev_b, w1_ref[:, pl.ds(c0, MLP_CHUNK)]), 0.0)
        return ff + _dot((hid * hid).astype(BF16), w2_ref[pl.ds(c0, MLP_CHUNK), :])

    ff = lax.fori_loop(0, D_FF // MLP_CHUNK, mlp_chunk, jnp.zeros_like(prev))
    o_ref[...] = _layer_norm(alpha * prev + ff, ln2_g_ref[...], ln2_b_ref[...])
    y_c, y_b, *y_a = _interleave(gens)
    y_all = jnp.concatenate([y.astype(BF16) for y in (*y_a, y_b, y_c)], axis=1)
    h1_prev[...] = _layer_norm(alpha * h + _dot(y_all, w_out_ref[...]), ln1_g_ref[...], ln1_b_ref[...])


def _layer_spec(arr, layer):
    zeros = (0,) * (arr.ndim - 1)
    return pl.BlockSpec((None,) + arr.shape[1:], lambda i: (layer,) + zeros, pipeline_mode=pl.Buffered(1))


def _layer_call(h_in, front, params, layer, alpha, front_pad, blocks_per_seq, n_seq, first, last):
    rows = ROW_BLOCK
    d = h_in.shape[1]
    n_blocks = n_seq * blocks_per_seq

    def unpadded(m):
        return (m // blocks_per_seq) * (blocks_per_seq - 1) + jnp.maximum(m % blocks_per_seq - 1, 0)

    def in_map(i):
        m = jnp.minimum(i, n_blocks - 1)
        return (unpadded(m) if first else m, 0)

    def out_map(i):
        m = jnp.maximum(i - 1, 0)
        return (unpadded(m) if last else m, 0)

    out_rows = (n_blocks - n_seq if last else n_blocks) * rows
    assert h_in.shape[0] == (n_blocks - n_seq if first else n_blocks) * rows
    return pl.pallas_call(
        functools.partial(_layer_kernel, alpha, front_pad, blocks_per_seq, first),
        grid=(n_blocks + 1,),
        in_specs=[pl.BlockSpec((rows, d), in_map), pl.BlockSpec((rows, d), lambda i: (0, 0)),
                  pl.BlockSpec((None, d, O_U), lambda i: (layer, 0, 0), pipeline_mode=pl.Buffered(1))]
        + [_layer_spec(p, layer) for p in params[1:]],
        out_specs=pl.BlockSpec((rows, d), out_map),
        out_shape=jax.ShapeDtypeStruct((out_rows, d), F32),
        scratch_shapes=[
            pltpu.VMEM((A_HEADS // 2, LANE, LANE), F32),
            pltpu.VMEM((SUBLANE, XBC_W), F32),
            pltpu.VMEM((B_G * B_N, B_W), F32),
            pltpu.VMEM((2, 1, C_S), F32),
            pltpu.VMEM((rows, d), F32),
        ],
        compiler_params=pltpu.CompilerParams(
            dimension_semantics=("arbitrary",), vmem_limit_bytes=VMEM_LIMIT),
        name="layer",
    )(h_in, front, *params)


def _block_diag(blocks):
    g, r, c = blocks.shape
    tiled = jnp.tile(blocks.reshape(g * r, c), (1, g))
    rg = lax.broadcasted_iota(jnp.int32, (g * r, g * c), 0) // r
    cg = lax.broadcasted_iota(jnp.int32, (g * r, g * c), 1) // c
    return jnp.where(rg == cg, tiled, 0.0)


def _s5_params(a_re, a_im, log_dt, b_re, b_im, c_re, c_im):
    lam_re = jnp.minimum(a_re, S5_MAX_RE)
    lam_im = a_im
    dt = jnp.exp(log_dt)[:, None]
    mag = jnp.exp(lam_re * dt)
    lb_re = mag * jnp.cos(lam_im * dt)
    lb_im = mag * jnp.sin(lam_im * dt)
    den = jnp.square(lam_re) + jnp.square(lam_im)
    nr = lb_re - 1.0
    s_re = (nr * lam_re + lb_im * lam_im) / den
    s_im = (lb_im * lam_re - nr * lam_im) / den
    bb_re = s_re[..., None] * b_re - s_im[..., None] * b_im
    bb_im = s_re[..., None] * b_im + s_im[..., None] * b_re
    steps = jnp.arange(1, SUBLANE + 1, dtype=F32)[:, None, None]
    pmag = jnp.exp(lam_re * dt * steps)
    pw_re = pmag * jnp.cos(lam_im * dt * steps)
    pw_im = pmag * jnp.sin(lam_im * dt * steps)
    quads = C_G // 4
    b_rows = []
    for d in range(S5_FOLD):
        if d == 0:
            d_re, d_im = bb_re, bb_im
        else:
            pr, pi = pw_re[d - 1][..., None], pw_im[d - 1][..., None]
            d_re, d_im = bb_re * pr - bb_im * pi, bb_re * pi + bb_im * pr
        per_quad = lambda x: jax.vmap(_block_diag)(x.transpose(0, 2, 1).reshape(quads, 4, C_CH, C_N))
        b_rows.append(jnp.concatenate([per_quad(d_re), per_quad(d_im)], axis=2))
    b_bd = jnp.concatenate(b_rows, axis=1)
    c_bd = jnp.concatenate([_block_diag(c_re.transpose(0, 2, 1)), -_block_diag(c_im.transpose(0, 2, 1))], axis=0)
    return b_bd.astype(BF16), c_bd.astype(BF16), pw_re.reshape(SUBLANE, C_S), pw_im.reshape(SUBLANE, C_S)


def kernel(x, meta_tokens, w_in, hgrn_lb_logits, hgrn_norm_w, m2_conv_w, m2_conv_b, m2_dt_bias, m2_a_log, m2_d, m2_norm_w, s5_a_re, s5_a_im, s5_log_dt, s5_b_re, s5_b_im, s5_c_re, s5_c_im, s5_d, s5_glu_w, s5_glu_b, w_out, ln1_g, ln1_b, w_mlp_in, w_mlp_out, ln2_g, ln2_b):
    bsz, seq, d = x.shape
    depth = w_in.shape[0]
    alpha = (2 * depth) ** 0.25
    assert seq % ROW_BLOCK == 0
    front_pad = ROW_BLOCK - N_META
    front = jnp.concatenate([jnp.zeros((front_pad, d), x.dtype), meta_tokens.astype(x.dtype)], axis=0)

    f32 = lambda v: v.astype(F32)
    row3 = lambda v: f32(v).reshape(depth, 1, -1)
    lb_cum = jnp.cumsum(jax.nn.softmax(f32(hgrn_lb_logits), axis=0), axis=0)
    lower = lb_cum - lb_cum[0]
    pad_cols = D_IN_PACKED - O_DT - B_HEADS
    src_dt, src_u = O_U, O_U + B_HEADS
    w_tail = jnp.concatenate([w_in[:, :, src_u:src_u + C_W], w_in[:, :, src_dt:src_u],
                              jnp.zeros((depth, d, pad_cols), w_in.dtype)], axis=2).astype(BF16)
    b_bd, c_bd, pow_re, pow_im = jax.vmap(_s5_params)(
        f32(s5_a_re), f32(s5_a_im), f32(s5_log_dt), f32(s5_b_re), f32(s5_b_im), f32(s5_c_re), f32(s5_c_im))
    pad6 = lambda v: jnp.concatenate([f32(v), jnp.zeros((depth, LANE - B_HEADS), F32)], axis=1).reshape(depth, 1, LANE)
    params = [
        w_in.astype(BF16), w_tail, row3(jnp.log(lower)), row3(1.0 - lower), row3(jnp.tile(hgrn_norm_w, (1, A_HEADS))),
        f32(m2_conv_w), row3(m2_conv_b), pad6(m2_dt_bias), pad6(m2_a_log),
        row3(jnp.repeat(m2_d, B_P, axis=1)), row3(m2_norm_w),
        b_bd, c_bd, pow_re, pow_im, row3(s5_d), s5_glu_w.astype(BF16), row3(s5_glu_b),
        w_out.astype(BF16), row3(ln1_g), row3(ln1_b),
        w_mlp_in.astype(BF16), w_mlp_out.astype(BF16), row3(ln2_g), row3(ln2_b),
    ]

    h2 = x.reshape(bsz * seq, d)
    for l in range(depth):
        h2 = _layer_call(h2, front, params, l, alpha, front_pad, seq // ROW_BLOCK + 1, bsz,
                         first=l == 0, last=l == depth - 1)
    return h2.reshape(bsz, seq, d)
```
